```python
import jax, jax.numpy as jnp
from jax import lax
import numpy as np


D_MODEL = 2048
BATCH = 16
SEQ = 2048
DEPTH = 1
DEC_BATCH = 4
DEC_SEQ = 4096
PAST_LEN = 128

GRID_W = 64
HEAD_DIM = 64
RWKV_HEADS = 16
NAT_HEADS = 16
RWKV_WIDTH = RWKV_HEADS * HEAD_DIM
NAT_WIDTH = NAT_HEADS * HEAD_DIM
D_MIX = RWKV_WIDTH + NAT_WIDTH
DECAY_RANK = 64
ICLR_RANK = 64
GATE_RANK = 128
RWKV_COLS = 3 * RWKV_WIDTH + DECAY_RANK + ICLR_RANK + GATE_RANK
IN_COLS = RWKV_COLS + 3 * NAT_WIDTH
D_FF = 5632
WIN_H = 8
WIN_W = 16
RMS_EPS = 1e-6
GN_EPS = 64e-5
DECAY_OFFSET = 0.5
FFN_RESIDUAL = 0.5

kernel_name = 'bi_rwkv7_natten2d_macaron_encoder'


def rmsnorm(x, g):
    xf = x.astype(jnp.float32)
    y = xf * lax.rsqrt(jnp.mean(xf * xf, axis=-1, keepdims=True) + RMS_EPS) * g.astype(jnp.float32)
    return y.astype(x.dtype)


def swiglu(h, w_gate, w_up, w_down):
    return (jax.nn.silu(h @ w_gate) * (h @ w_up)) @ w_down


def centred_shift(u, mu):
    prev = jnp.pad(u[:, :-1], ((0, 0), (1, 0), (0, 0)))
    nxt = jnp.pad(u[:, 1:], ((0, 0), (0, 1), (0, 0)))
    return u + mu * (0.5 * (prev + nxt) - u)


def wkv7_scan(r, w, k, v, a, b, reverse):
    B, T, H, N = r.shape

    def step(S, inp):
        r_t, w_t, k_t, v_t, a_t, b_t = inp
        sa = jnp.einsum('bhij,bhj->bhi', S, a_t)
        S = S * w_t[:, :, None, :] + sa[..., :, None] * b_t[:, :, None, :] + v_t[..., :, None] * k_t[:, :, None, :]
        return S, jnp.einsum('bhij,bhj->bhi', S, r_t)

    xs = (jnp.moveaxis(r, 1, 0), jnp.moveaxis(w, 1, 0), jnp.moveaxis(k, 1, 0),
          jnp.moveaxis(v, 1, 0), jnp.moveaxis(a, 1, 0), jnp.moveaxis(b, 1, 0))
    S0 = jnp.zeros((B, H, N, N), jnp.float32)
    _, ys = lax.scan(step, S0, xs, reverse=reverse)
    return jnp.moveaxis(ys, 0, 1)


def rwkv7_time_mix(u, decay_bias_fwd, decay_up_fwd, decay_bias_bwd, decay_up_bwd,
                   iclr_bias_fwd, iclr_up_fwd, iclr_bias_bwd, iclr_up_bwd, gate_up,
                   key_norm_scale, key_iclr_mix, bonus_scale, gn_w, gn_b):
    B, T, _ = u.shape
    u = u.astype(jnp.float32)
    W = RWKV_WIDTH
    r = u[..., :W]
    k = u[..., W:2 * W]
    v = u[..., 2 * W:3 * W]
    o = 3 * W
    dec_lo = jnp.tanh(u[..., o:o + DECAY_RANK])
    o = o + DECAY_RANK
    iclr_lo = u[..., o:o + ICLR_RANK]
    o = o + ICLR_RANK
    gate_lo = jax.nn.sigmoid(u[..., o:o + GATE_RANK])

    def heads(t):
        return t.reshape(B, T, RWKV_HEADS, HEAD_DIM)

    kk = heads(k * key_norm_scale)
    kk = kk / jnp.maximum(jnp.linalg.norm(kk, axis=-1, keepdims=True), 1e-12)

    def direction(decay_bias, decay_up, iclr_bias, iclr_up, reverse):
        log_w = -jax.nn.softplus(-(decay_bias + dec_lo @ decay_up)) - DECAY_OFFSET
        w = jnp.exp(-jnp.exp(log_w))
        a = jax.nn.sigmoid(iclr_bias + iclr_lo @ iclr_up)
        k_d = k * (1.0 + (a - 1.0) * key_iclr_mix)
        y = wkv7_scan(heads(r), heads(w), heads(k_d), heads(v), -kk, kk * heads(a), reverse)
        return y, k_d

    y_f, k_f = direction(decay_bias_fwd, decay_up_fwd, iclr_bias_fwd, iclr_up_fwd, False)
    y_b, k_b = direction(decay_bias_bwd, decay_up_bwd, iclr_bias_bwd, iclr_up_bwd, True)
    y = y_f + y_b
    mean = jnp.mean(y, axis=-1, keepdims=True)
    var = jnp.var(y, axis=-1, keepdims=True)
    y = ((y - mean) * lax.rsqrt(var + GN_EPS)).reshape(B, T, W) * gn_w + gn_b
    bonus = jnp.sum(heads(r * (0.5 * (k_f + k_b)) * bonus_scale), axis=-1, keepdims=True) * heads(v)
    return (y + bonus.reshape(B, T, W)) * (gate_lo @ gate_up)


def neighbourhood_attention(q, k, v, rpb):
    B, T, H, Dh = q.shape
    rows = T // GRID_W
    kh = min(WIN_H, rows)
    qg = q.reshape(B, rows, GRID_W, H, Dh)
    kg = k.reshape(B, rows, GRID_W, H, Dh)
    vg = v.reshape(B, rows, GRID_W, H, Dh)
    cols = jnp.arange(GRID_W)
    col_start = jnp.clip(cols - WIN_W // 2, 0, GRID_W - WIN_W)
    col_idx = col_start[:, None] + jnp.arange(WIN_W)[None, :]
    col_rel = col_idx - cols[:, None] + (WIN_W - 1)
    scale = Dh ** -0.5
    rpb = rpb.astype(jnp.float32)

    def row_block(i):
        row_start = jnp.clip(i - kh // 2, 0, rows - kh)
        q_i = lax.dynamic_index_in_dim(qg, i, axis=1, keepdims=False)
        k_rows = lax.dynamic_slice_in_dim(kg, row_start, kh, axis=1)
        v_rows = lax.dynamic_slice_in_dim(vg, row_start, kh, axis=1)
        k_nb = k_rows[:, :, col_idx]
        v_nb = v_rows[:, :, col_idx]
        s = jnp.einsum('bqhd,baqchd->bhqac', q_i, k_nb).astype(jnp.float32) * scale
        row_rel = row_start + jnp.arange(kh) - i + (WIN_H - 1)
        bias = rpb[:, row_rel[:, None, None], col_rel[None, :, :]]
        s = s + jnp.transpose(bias, (0, 2, 1, 3))[None]
        p = jax.nn.softmax(s.reshape(B, H, GRID_W, kh * WIN_W), axis=-1).reshape(B, H, GRID_W, kh, WIN_W)
        return jnp.einsum('bhqac,baqchd->bqhd', p.astype(v.dtype), v_nb)

    out = lax.map(row_block, jnp.arange(rows))
    return jnp.moveaxis(out, 0, 1).reshape(B, T, H * Dh)


def encoder_layer(x, ffn1_pre_g, ffn1_post_g, ffn1_w_gate, ffn1_w_up, ffn1_w_down,
                  mix_pre_g, w_in, rwkv_shift_mix,
                  decay_bias_fwd, decay_up_fwd, decay_bias_bwd, decay_up_bwd,
                  iclr_bias_fwd, iclr_up_fwd, iclr_bias_bwd, iclr_up_bwd, gate_up,
                  key_norm_scale, key_iclr_mix, bonus_scale, gn_w, gn_b,
                  nat_rpb, w_out, mix_post_g,
                  ffn2_pre_g, ffn2_post_g, ffn2_w_gate, ffn2_w_up, ffn2_w_down):
    B, T, _ = x.shape
    h = rmsnorm(x, ffn1_pre_g)
    x = x + FFN_RESIDUAL * rmsnorm(swiglu(h, ffn1_w_gate, ffn1_w_up, ffn1_w_down), ffn1_post_g)
    h = rmsnorm(x, mix_pre_g)
    z = h @ w_in
    z_rwkv = centred_shift(z[..., :RWKV_COLS], rwkv_shift_mix)
    o_rwkv = rwkv7_time_mix(z_rwkv, decay_bias_fwd, decay_up_fwd, decay_bias_bwd, decay_up_bwd,
                            iclr_bias_fwd, iclr_up_fwd, iclr_bias_bwd, iclr_up_bwd, gate_up,
                            key_norm_scale, key_iclr_mix, bonus_scale, gn_w, gn_b).astype(x.dtype)
    z_nat = z[..., RWKV_COLS:]
    q = z_nat[..., :NAT_WIDTH].reshape(B, T, NAT_HEADS, HEAD_DIM)
    k = z_nat[..., NAT_WIDTH:2 * NAT_WIDTH].reshape(B, T, NAT_HEADS, HEAD_DIM)
    v = z_nat[..., 2 * NAT_WIDTH:].reshape(B, T, NAT_HEADS, HEAD_DIM)
    o_nat = neighbourhood_attention(q, k, v, nat_rpb).astype(x.dtype)
    o = jnp.concatenate([o_rwkv, o_nat], axis=-1) @ w_out
    x = x + rmsnorm(o, mix_post_g)
    h = rmsnorm(x, ffn2_pre_g)
    x = x + FFN_RESIDUAL * rmsnorm(swiglu(h, ffn2_w_gate, ffn2_w_up, ffn2_w_down), ffn2_post_g)
    return x


def run_trunk(x, ffn1_pre_g, ffn1_post_g, ffn1_w_gate, ffn1_w_up, ffn1_w_down,
              mix_pre_g, w_in, rwkv_shift_mix,
              decay_bias_fwd, decay_up_fwd, decay_bias_bwd, decay_up_bwd,
              iclr_bias_fwd, iclr_up_fwd, iclr_bias_bwd, iclr_up_bwd, gate_up,
              key_norm_scale, key_iclr_mix, bonus_scale, gn_w, gn_b,
              nat_rpb, w_out, mix_post_g,
              ffn2_pre_g, ffn2_post_g, ffn2_w_gate, ffn2_w_up, ffn2_w_down):
    for l in range(DEPTH):
        x = encoder_layer(x, ffn1_pre_g[l], ffn1_post_g[l], ffn1_w_gate[l], ffn1_w_up[l], ffn1_w_down[l],
                          mix_pre_g[l], w_in[l], rwkv_shift_mix[l],
                          decay_bias_fwd[l], decay_up_fwd[l], decay_bias_bwd[l], decay_up_bwd[l],
                          iclr_bias_fwd[l], iclr_up_fwd[l], iclr_bias_bwd[l], iclr_up_bwd[l], gate_up[l],
                          key_norm_scale[l], key_iclr_mix[l], bonus_scale[l], gn_w[l], gn_b[l],
                          nat_rpb[l], w_out[l], mix_post_g[l],
                          ffn2_pre_g[l], ffn2_post_g[l], ffn2_w_gate[l], ffn2_w_up[l], ffn2_w_down[l])
    return x


def setup_inputs(seed: int = 0) -> dict:
    key = jax.random.key(seed)
    ks = jax.random.split(key, 32)
    f32 = jnp.float32

    def nrm(k, shape, scale):
        return jax.random.normal(k, shape, f32) * scale

    def gain(k, shape):
        return 1.0 + 0.02 * jax.random.normal(k, shape, f32)

    D, W = D_MODEL, RWKV_WIDTH
    return {
        'x_prompt': nrm(ks[0], (BATCH, SEQ, D), 1.0),
        'x_sample': nrm(ks[1], (DEC_BATCH, DEC_SEQ, D), 1.0),
        'ffn1_pre_g': gain(ks[2], (DEPTH, D)),
        'ffn1_post_g': gain(ks[3], (DEPTH, D)),
        'ffn1_w_gate': nrm(ks[4], (DEPTH, D, D_FF), D ** -0.5),
        'ffn1_w_up': nrm(ks[5], (DEPTH, D, D_FF), D ** -0.5),
        'ffn1_w_down': nrm(ks[6], (DEPTH, D_FF, D), D_FF ** -0.5),
        'mix_pre_g': gain(ks[7], (DEPTH, D)),
        'w_in': nrm(ks[8], (DEPTH, D, IN_COLS), D ** -0.5),
        'rwkv_shift_mix': jax.random.uniform(ks[9], (DEPTH, RWKV_COLS), f32),
        'decay_bias_fwd': jax.random.uniform(ks[10], (DEPTH, W), f32, -6.0, 1.0),
        'decay_up_fwd': nrm(ks[11], (DEPTH, DECAY_RANK, W), 0.1),
        'decay_bias_bwd': jax.random.uniform(ks[12], (DEPTH, W), f32, -6.0, 1.0),
        'decay_up_bwd': nrm(ks[13], (DEPTH, DECAY_RANK, W), 0.1),
        'iclr_bias_fwd': nrm(ks[14], (DEPTH, W), 0.5),
        'iclr_up_fwd': nrm(ks[15], (DEPTH, ICLR_RANK, W), ICLR_RANK ** -0.5),
        'iclr_bias_bwd': nrm(ks[16], (DEPTH, W), 0.5),
        'iclr_up_bwd': nrm(ks[17], (DEPTH, ICLR_RANK, W), ICLR_RANK ** -0.5),
        'gate_up': nrm(ks[18], (DEPTH, GATE_RANK, W), GATE_RANK ** -0.5),
        'key_norm_scale': 0.85 + 0.02 * jax.random.normal(ks[19], (DEPTH, W), f32),
        'key_iclr_mix': gain(ks[20], (DEPTH, W)),
        'bonus_scale': nrm(ks[21], (DEPTH, W), 0.1),
        'gn_w': gain(ks[22], (DEPTH, W)),
        'gn_b': nrm(ks[23], (DEPTH, W), 0.02),
        'nat_rpb': nrm(ks[24], (DEPTH, NAT_HEADS, 2 * WIN_H - 1, 2 * WIN_W - 1), 0.1),
        'w_out': nrm(ks[25], (DEPTH, D_MIX, D), D_MIX ** -0.5),
        'mix_post_g': gain(ks[26], (DEPTH, D)),
        'ffn2_pre_g': gain(ks[27], (DEPTH, D)),
        'ffn2_post_g': gain(ks[28], (DEPTH, D)),
        'ffn2_w_gate': nrm(ks[29], (DEPTH, D, D_FF), D ** -0.5),
        'ffn2_w_up': nrm(ks[30], (DEPTH, D, D_FF), D ** -0.5),
        'ffn2_w_down': nrm(ks[31], (DEPTH, D_FF, D), D_FF ** -0.5),
    }


def reference(x_prompt, x_sample, ffn1_pre_g, ffn1_post_g, ffn1_w_gate, ffn1_w_up, ffn1_w_down,
              mix_pre_g, w_in, rwkv_shift_mix,
              decay_bias_fwd, decay_up_fwd, decay_bias_bwd, decay_up_bwd,
              iclr_bias_fwd, iclr_up_fwd, iclr_bias_bwd, iclr_up_bwd, gate_up,
              key_norm_scale, key_iclr_mix, bonus_scale, gn_w, gn_b,
              nat_rpb, w_out, mix_post_g,
              ffn2_pre_g, ffn2_post_g, ffn2_w_gate, ffn2_w_up, ffn2_w_down):
    y_prompt = run_trunk(x_prompt, ffn1_pre_g, ffn1_post_g, ffn1_w_gate, ffn1_w_up, ffn1_w_down,
                         mix_pre_g, w_in, rwkv_shift_mix,
                         decay_bias_fwd, decay_up_fwd, decay_bias_bwd, decay_up_bwd,
                         iclr_bias_fwd, iclr_up_fwd, iclr_bias_bwd, iclr_up_bwd, gate_up,
                         key_norm_scale, key_iclr_mix, bonus_scale, gn_w, gn_b,
                         nat_rpb, w_out, mix_post_g,
                         ffn2_pre_g, ffn2_post_g, ffn2_w_gate, ffn2_w_up, ffn2_w_down)
    y_sample = run_trunk(x_sample, ffn1_pre_g, ffn1_post_g, ffn1_w_gate, ffn1_w_up, ffn1_w_down,
                         mix_pre_g, w_in, rwkv_shift_mix,
                         decay_bias_fwd, decay_up_fwd, decay_bias_bwd, decay_up_bwd,
                         iclr_bias_fwd, iclr_up_fwd, iclr_bias_bwd, iclr_up_bwd, gate_up,
                         key_norm_scale, key_iclr_mix, bonus_scale, gn_w, gn_b,
                         nat_rpb, w_out, mix_post_g,
                         ffn2_pre_g, ffn2_post_g, ffn2_w_gate, ffn2_w_up, ffn2_w_down)
    return (y_prompt, y_sample)
```

```python
import functools
import math

import jax
import jax.numpy as jnp
from jax import lax
from jax.experimental import pallas as pl
from jax.experimental.pallas import tpu as pltpu

F32 = jnp.float32
BF16 = jnp.bfloat16
HIGHEST = lax.Precision.HIGHEST

HEAD_DIM = 64
GRID_W = 64
WIN_H = 8
WIN_W = 16
DECAY_RANK = 64
ICLR_RANK = 64
GATE_RANK = 128
RMS_EPS = 1e-6
GN_EPS = 64e-5
DECAY_OFFSET = 0.5
FFN_RESIDUAL = 0.5
MASK_VALUE = -1e30

LANES = 128
SUBLANES = 8
VMEM_LIMIT_BYTES = 56 * 1024 * 1024

PAIR = 2 * HEAD_DIM
CHUNK = 64
STACK = 2 * CHUNK


def _params(*semantics):
    return pltpu.CompilerParams(dimension_semantics=semantics, vmem_limit_bytes=VMEM_LIMIT_BYTES)


def _dot(a, b, precision=None):
    return jnp.dot(a, b, preferred_element_type=F32, precision=precision)


def _dot_nt(a, b):
    return lax.dot_general(a, b, (((1,), (1,)), ((), ())), preferred_element_type=F32)


def _dot_tn(a, b):
    return lax.dot_general(a, b, (((0,), (0,)), ((), ())), preferred_element_type=F32)


def _rms(x, g):
    return x * lax.rsqrt(jnp.mean(x * x, axis=-1, keepdims=True) + RMS_EPS) * g


def _pick_tile(n, target):
    t = min(n, target)
    while n % t:
        t //= 2
    return t


def _ffn_kernel(x_ref, gpre_ref, gpost_ref, wg_ref, wu_ref, wd_ref, o_ref, h_ref, acc_ref):
    j = pl.program_id(1)

    @pl.when(j == 0)
    def _():
        h_ref[...] = _rms(x_ref[...], gpre_ref[...]).astype(BF16)
        acc_ref[...] = jnp.zeros_like(acc_ref)

    h = h_ref[...]
    g = _dot(h, wg_ref[...])
    u = _dot(h, wu_ref[...])
    act = (g * jax.nn.sigmoid(g) * u).astype(BF16)
    acc_ref[...] += _dot(act, wd_ref[...])

    @pl.when(j == pl.num_programs(1) - 1)
    def _():
        o_ref[...] = x_ref[...] + FFN_RESIDUAL * _rms(acc_ref[...], gpost_ref[...])


def _ffn(x, g_pre, g_post, w_gate, w_up, w_down):
    n, d = x.shape
    f = w_gate.shape[1]
    tm = _pick_tile(n, 512)
    tf = _pick_tile(f, 512)
    return pl.pallas_call(
        _ffn_kernel,
        grid=(n // tm, f // tf),
        in_specs=[
            pl.BlockSpec((tm, d), lambda i, j: (i, 0)),
            pl.BlockSpec((1, d), lambda i, j: (0, 0)),
            pl.BlockSpec((1, d), lambda i, j: (0, 0)),
            pl.BlockSpec((d, tf), lambda i, j: (0, j)),
            pl.BlockSpec((d, tf), lambda i, j: (0, j)),
            pl.BlockSpec((tf, d), lambda i, j: (j, 0)),
        ],
        out_specs=pl.BlockSpec((tm, d), lambda i, j: (i, 0)),
        out_shape=jax.ShapeDtypeStruct((n, d), F32),
        scratch_shapes=[pltpu.VMEM((tm, d), BF16), pltpu.VMEM((tm, d), F32)],
        compiler_params=_params("parallel", "arbitrary"),
    )(x, g_pre, g_post, w_gate, w_up, w_down)


def _norm_proj_kernel(x_ref, g_ref, w_ref, o_ref):
    h = _rms(x_ref[...], g_ref[...]).astype(BF16)
    o_ref[...] = _dot(h, w_ref[...]).astype(o_ref.dtype)


def _norm_proj(x, g, w, out_dtype):
    n, d = x.shape
    c = w.shape[1]
    tm = _pick_tile(n, 256)
    return pl.pallas_call(
        _norm_proj_kernel,
        grid=(n // tm,),
        in_specs=[
            pl.BlockSpec((tm, d), lambda i: (i, 0)),
            pl.BlockSpec((1, d), lambda i: (0, 0)),
            pl.BlockSpec((d, c), lambda i: (0, 0)),
        ],
        out_specs=pl.BlockSpec((tm, c), lambda i: (i, 0)),
        out_shape=jax.ShapeDtypeStruct((n, c), out_dtype),
        compiler_params=_params("parallel"),
    )(x, g, w)


def _rwkv_prep_kernel(seq_len, z_ref, zp_ref, zn_ref, mu_ref,
                      dbf_ref, duf_ref, dbb_ref, dub_ref, ibf_ref, iuf_ref, ibb_ref, iub_ref, gu_ref,
                      r_ref, k_ref, v_ref, lwf_ref, lwb_ref, af_ref, ab_ref, g_ref, u_ref):
    i = pl.program_id(0)
    tm = z_ref.shape[0]
    w = r_ref.shape[1]
    z = z_ref[...]
    row = lax.broadcasted_iota(jnp.int32, (tm, 1), 0)
    at_seq_start = (i * tm) % seq_len == 0
    at_seq_end = ((i + 1) * tm) % seq_len == 0
    prev_row = jnp.where(at_seq_start, 0.0, zp_ref[SUBLANES - 1:SUBLANES, :])
    next_row = jnp.where(at_seq_end, 0.0, zn_ref[0:1, :])
    prev = jnp.where(row == 0, prev_row, pltpu.roll(z, 1, axis=0))
    nxt = jnp.where(row == tm - 1, next_row, pltpu.roll(z, tm - 1, axis=0))
    u_ref[...] = z + mu_ref[...] * (0.5 * (prev + nxt) - z)

    r_ref[...] = u_ref[:, 0:w]
    k_ref[...] = u_ref[:, w:2 * w]
    v_ref[...] = u_ref[:, 2 * w:3 * w]
    o = 3 * w
    dec_lo = jnp.tanh(u_ref[:, o:o + DECAY_RANK])
    o += DECAY_RANK
    iclr_lo = u_ref[:, o:o + ICLR_RANK]
    o += ICLR_RANK
    gate_lo = jax.nn.sigmoid(u_ref[:, o:o + GATE_RANK])

    scale = -math.exp(-DECAY_OFFSET)
    lwf_ref[...] = scale * jax.nn.sigmoid(dbf_ref[...] + _dot(dec_lo, duf_ref[...], HIGHEST))
    lwb_ref[...] = scale * jax.nn.sigmoid(dbb_ref[...] + _dot(dec_lo, dub_ref[...], HIGHEST))
    af_ref[...] = jax.nn.sigmoid(ibf_ref[...] + _dot(iclr_lo, iuf_ref[...], HIGHEST))
    ab_ref[...] = jax.nn.sigmoid(ibb_ref[...] + _dot(iclr_lo, iub_ref[...], HIGHEST))
    g_ref[...] = _dot(gate_lo, gu_ref[...], HIGHEST)


def _rwkv_prep(z, seq_len, mu, dbf, duf, dbb, dub, ibf, iuf, ibb, iub, gate_up):
    n, c = z.shape
    w = duf.shape[1]
    tm = _pick_tile(seq_len, 256)
    hb = tm // SUBLANES
    last_hb = n // SUBLANES - 1
    row_spec = lambda cols: pl.BlockSpec((1, cols), lambda i: (0, 0))
    mat_spec = lambda rows: pl.BlockSpec((rows, w), lambda i: (0, 0))
    out_spec = pl.BlockSpec((tm, w), lambda i: (i, 0))
    out = jax.ShapeDtypeStruct((n, w), F32)
    return pl.pallas_call(
        functools.partial(_rwkv_prep_kernel, seq_len),
        grid=(n // tm,),
        in_specs=[
            pl.BlockSpec((tm, c), lambda i: (i, 0)),
            pl.BlockSpec((SUBLANES, c), lambda i: (jnp.maximum(i * hb - 1, 0), 0)),
            pl.BlockSpec((SUBLANES, c), lambda i: (jnp.minimum((i + 1) * hb, last_hb), 0)),
            row_spec(c),
            row_spec(w), mat_spec(DECAY_RANK), row_spec(w), mat_spec(DECAY_RANK),
            row_spec(w), mat_spec(ICLR_RANK), row_spec(w), mat_spec(ICLR_RANK),
            mat_spec(GATE_RANK),
        ],
        out_specs=[out_spec] * 8,
        out_shape=[out] * 8,
        scratch_shapes=[pltpu.VMEM((tm, c), F32)],
        compiler_params=_params("parallel"),
    )(z, z, z, mu, dbf, duf, dbb, dub, ibf, iuf, ibb, iub, gate_up)


def _stack(x, lane_lo):
    return jnp.concatenate([jnp.where(lane_lo, x, 0.0), jnp.where(lane_lo, 0.0, x)], axis=0)


def _wkv_kernel(reverse, r_ref, k_ref, v_ref, lw_ref, al_ref, kns_ref, mix_ref, y_ref, s_ref):
    tc = pl.program_id(1)
    n_sub = r_ref.shape[0] // CHUNK
    n_pair = r_ref.shape[1] // PAIR

    @pl.when(tc == 0)
    def _():
        s_ref[...] = jnp.zeros_like(s_ref)

    ii = lax.broadcasted_iota(jnp.int32, (STACK, STACK), 0)
    jj = lax.broadcasted_iota(jnp.int32, (STACK, STACK), 1)
    ti = ii % CHUNK
    tj = jj % CHUNK
    same_head = (ii // CHUNK) == (jj // CHUNK)
    if reverse:
        strict = same_head & (ti < tj)
        incl = same_head & (ti <= tj)
    else:
        strict = same_head & (ti > tj)
        incl = same_head & (ti >= tj)
    ci = lax.broadcasted_iota(jnp.int32, (CHUNK, CHUNK), 0)
    cj = lax.broadcasted_iota(jnp.int32, (CHUNK, CHUNK), 1)
    cum_tri = jnp.where((ci <= cj) if reverse else (ci >= cj), 1.0, 0.0).astype(F32)
    eye = jnp.where(ii == jj, 1.0, 0.0).astype(F32)
    head_ones = jnp.where(
        (lax.broadcasted_iota(jnp.int32, (PAIR, PAIR), 0) // HEAD_DIM)
        == (lax.broadcasted_iota(jnp.int32, (PAIR, PAIR), 1) // HEAD_DIM), 1.0, 0.0).astype(F32)
    lane_lo = lax.broadcasted_iota(jnp.int32, (CHUNK, PAIR), 1) < HEAD_DIM
    end_row = 0 if reverse else CHUNK - 1

    level_masks = []
    m = 1
    while m < CHUNK:
        level_masks.append(strict & (ti // (2 * m) == tj // (2 * m)) & (ti // m != tj // m))
        m *= 2

    def sub_chunk(step, carry):
        c = (n_sub - 1 - step) if reverse else step
        rows = pl.ds(pl.multiple_of(c * CHUNK, CHUNK), CHUNK)
        for p in range(n_pair):
            lanes = slice(p * PAIR, (p + 1) * PAIR)
            r = r_ref[rows, lanes]
            k = k_ref[rows, lanes]
            v = v_ref[rows, lanes]
            lw = lw_ref[rows, lanes]
            al = al_ref[rows, lanes]

            kn = k * kns_ref[:, lanes]
            norm = jnp.sqrt(_dot(kn * kn, head_ones, HIGHEST))
            kk = kn / jnp.maximum(norm, 1e-12)
            kd = k * (1.0 + (al - 1.0) * mix_ref[:, lanes])
            b = kk * al

            cum = _dot(cum_tri, lw, HIGHEST)
            tot = cum[end_row:end_row + 1, :]
            e_in = jnp.exp(cum)
            e_ex = jnp.exp(cum - lw)
            e_inv = jnp.exp(-cum)
            e_end = jnp.exp(tot - cum)

            rt = _stack(r * e_in, lane_lo).astype(BF16)
            at = _stack(-kk * e_ex, lane_lo).astype(BF16)
            bt = _stack(b * e_inv, lane_lo).astype(BF16)
            kt = _stack(kd * e_inv, lane_lo).astype(BF16)
            bh = _stack(b * e_end, lane_lo).astype(BF16)
            kh = _stack(kd * e_end, lane_lo).astype(BF16)
            vs = _stack(v, lane_lo).astype(BF16)

            m1 = _dot_nt(jnp.concatenate([at, rt], axis=0), jnp.concatenate([bt, kt], axis=0))
            l_ab = m1[:STACK, :STACK]
            l_ak = jnp.where(strict, m1[:STACK, STACK:], 0.0).astype(BF16)
            a_rb = jnp.where(incl, m1[STACK:, :STACK], 0.0).astype(BF16)
            a_rk = jnp.where(incl, m1[STACK:, STACK:], 0.0).astype(BF16)

            inv = eye + jnp.where(level_masks[0], l_ab, 0.0)
            for mask in level_masks[1:]:
                l_m = jnp.where(mask, l_ab, 0.0).astype(BF16)
                inv_b = inv.astype(BF16)
                inv = inv + _dot(inv_b, _dot(l_m, inv_b).astype(BF16))
            inv_b = inv.astype(BF16)

            lakv = _dot(l_ak, vs).astype(BF16)
            wu = _dot(inv_b, jnp.concatenate([at, lakv], axis=1))
            wt = wu[:, :PAIR].astype(BF16)
            u0 = wu[:, PAIR:]

            s = s_ref[p]
            s_b = s.astype(BF16)
            ur = _dot_nt(jnp.concatenate([wt, rt], axis=0), s_b)
            u = ur[:STACK] + u0
            uv = jnp.concatenate([u.astype(BF16), vs], axis=0)
            ys = ur[STACK:] + _dot(jnp.concatenate([a_rb, a_rk], axis=1), uv)
            s_ref[p] = s * jnp.exp(tot) + _dot_tn(uv, jnp.concatenate([bh, kh], axis=0))
            y_ref[rows, lanes] = ys[:CHUNK] + ys[CHUNK:]
        return carry

    lax.fori_loop(0, n_sub, sub_chunk, 0)


def _wkv(r, k, v, lw, al, kns, mix, reverse):
    b, t, w = r.shape
    lc = _pick_tile(t, 256)
    n_t = t // lc
    tmap = (lambda bi, ti: (bi, n_t - 1 - ti, 0)) if reverse else (lambda bi, ti: (bi, ti, 0))
    seq_spec = pl.BlockSpec((None, lc, w), tmap)
    row_spec = pl.BlockSpec((1, w), lambda bi, ti: (0, 0))
    return pl.pallas_call(
        functools.partial(_wkv_kernel, reverse),
        grid=(b, n_t),
        in_specs=[seq_spec] * 5 + [row_spec] * 2,
        out_specs=seq_spec,
        out_shape=jax.ShapeDtypeStruct((b, t, w), F32),
        scratch_shapes=[pltpu.VMEM((w // PAIR, PAIR, PAIR), F32)],
        compiler_params=_params("parallel", "arbitrary"),
    )(r, k, v, lw, al, kns, mix)


def _rwkv_post_kernel(yf_ref, yb_ref, r_ref, k_ref, v_ref, af_ref, ab_ref, g_ref,
                      mix_ref, bonus_ref, gnw_ref, gnb_ref, o_ref):
    n_pair = o_ref.shape[1] // PAIR
    head_ones = jnp.where(
        (lax.broadcasted_iota(jnp.int32, (PAIR, PAIR), 0) // HEAD_DIM)
        == (lax.broadcasted_iota(jnp.int32, (PAIR, PAIR), 1) // HEAD_DIM), 1.0, 0.0).astype(F32)
    for p in range(n_pair):
        lanes = slice(p * PAIR, (p + 1) * PAIR)
        y = yf_ref[:, lanes] + yb_ref[:, lanes]
        mean = _dot(y, head_ones, HIGHEST) * (1.0 / HEAD_DIM)
        yc = y - mean
        var = _dot(yc * yc, head_ones, HIGHEST) * (1.0 / HEAD_DIM)
        yn = yc * lax.rsqrt(var + GN_EPS) * gnw_ref[:, lanes] + gnb_ref[:, lanes]
        k = k_ref[:, lanes]
        mix = mix_ref[:, lanes]
        k_f = k * (1.0 + (af_ref[:, lanes] - 1.0) * mix)
        k_b = k * (1.0 + (ab_ref[:, lanes] - 1.0) * mix)
        rk = r_ref[:, lanes] * (0.5 * (k_f + k_b)) * bonus_ref[:, lanes]
        bonus = _dot(rk, head_ones, HIGHEST) * v_ref[:, lanes]
        o_ref[:, lanes] = ((yn + bonus) * g_ref[:, lanes]).astype(o_ref.dtype)


def _rwkv_post(yf, yb, r, k, v, af, ab, g, mix, bonus_scale, gn_w, gn_b):
    n, w = yf.shape
    tm = _pick_tile(n, 256)
    tile = pl.BlockSpec((tm, w), lambda i: (i, 0))
    row = pl.BlockSpec((1, w), lambda i: (0, 0))
    return pl.pallas_call(
        _rwkv_post_kernel,
        grid=(n // tm,),
        in_specs=[tile] * 8 + [row] * 4,
        out_specs=tile,
        out_shape=jax.ShapeDtypeStruct((n, w), BF16),
        compiler_params=_params("parallel"),
    )(yf, yb, r, k, v, af, ab, g, mix, bonus_scale, gn_w, gn_b)


def _nat_bias_table(rpb):
    off = jnp.arange(WIN_H)
    row_rel = off[None, :] - off[:, None] + (WIN_H - 1)
    cols = jnp.arange(GRID_W)
    col_start = jnp.clip(cols - WIN_W // 2, 0, GRID_W - WIN_W)
    in_win = (cols[None, :] >= col_start[:, None]) & (cols[None, :] < col_start[:, None] + WIN_W)
    col_rel = jnp.clip(cols[None, :] - cols[:, None] + (WIN_W - 1), 0, 2 * WIN_W - 2)
    tbl = rpb.astype(F32)[:, row_rel[:, None, :, None], col_rel[None, :, None, :]]
    tbl = jnp.where(in_win[None, None, :, None, :], tbl, MASK_VALUE)
    return tbl.reshape(rpb.shape[0], WIN_H, GRID_W, WIN_H * GRID_W)


def _nat_row_start(i, n_rows):
    return jnp.clip(i - WIN_H // 2, 0, n_rows - WIN_H)


def _natten_kernel(n_rows, q_ref, k_ref, v_ref, bias_ref, o_ref):
    i = pl.program_id(1)
    n_pair = q_ref.shape[1] // PAIR
    n_keys = WIN_H * GRID_W
    key_rows = pl.ds(pl.multiple_of(_nat_row_start(i, n_rows) * GRID_W, GRID_W), n_keys)
    lane_lo = lax.broadcasted_iota(jnp.int32, (GRID_W, PAIR), 1) < HEAD_DIM
    scale = HEAD_DIM ** -0.5
    for p in range(n_pair):
        lanes = slice(p * PAIR, (p + 1) * PAIR)
        q = q_ref[:, lanes] * scale
        k = k_ref[key_rows, lanes]
        v = v_ref[key_rows, lanes]
        outs = []
        for half in range(2):
            q_h = jnp.where(lane_lo if half == 0 else ~lane_lo, q, jnp.zeros_like(q))
            s = _dot_nt(q_h, k) + bias_ref[2 * p + half, 0]
            e = jnp.exp(s - jnp.max(s, axis=-1, keepdims=True))
            denom = jnp.sum(e, axis=-1, keepdims=True)
            outs.append(_dot(e.astype(BF16), v) / denom)
        o_ref[:, lanes] = jnp.where(lane_lo, outs[0], outs[1]).astype(o_ref.dtype)


def _natten(qkv, bias_tbl):
    b, t, w3 = qkv.shape
    w = w3 // 3
    n_rows = t // GRID_W
    n_heads = w // HEAD_DIM
    bias_map = lambda bi, i: (0, i - _nat_row_start(i, n_rows), 0, 0)
    return pl.pallas_call(
        functools.partial(_natten_kernel, n_rows),
        grid=(b, n_rows),
        in_specs=[
            pl.BlockSpec((None, GRID_W, w), lambda bi, i: (bi, i, 0)),
            pl.BlockSpec((None, t, w), lambda bi, i: (bi, 0, 1)),
            pl.BlockSpec((None, t, w), lambda bi, i: (bi, 0, 2)),
            pl.BlockSpec((n_heads, 1, GRID_W, WIN_H * GRID_W), bias_map),
        ],
        out_specs=pl.BlockSpec((None, GRID_W, w), lambda bi, i: (bi, i, 0)),
        out_shape=jax.ShapeDtypeStruct((b, t, w), BF16),
        compiler_params=_params("parallel", "arbitrary"),
    )(qkv, qkv, qkv, bias_tbl)


def _out_proj_kernel(x_ref, oa_ref, ob_ref, wa_ref, wb_ref, g_ref, o_ref):
    o = _dot(oa_ref[...], wa_ref[...]) + _dot(ob_ref[...], wb_ref[...])
    o_ref[...] = x_ref[...] + _rms(o, g_ref[...])


def _out_proj(x, o_a, o_b, w_a, w_b, g):
    n, d = x.shape
    w = o_a.shape[1]
    tm = _pick_tile(n, 256)
    return pl.pallas_call(
        _out_proj_kernel,
        grid=(n // tm,),
        in_specs=[
            pl.BlockSpec((tm, d), lambda i: (i, 0)),
            pl.BlockSpec((tm, w), lambda i: (i, 0)),
            pl.BlockSpec((tm, w), lambda i: (i, 0)),
            pl.BlockSpec((w, d), lambda i: (0, 0)),
            pl.BlockSpec((w, d), lambda i: (0, 0)),
            pl.BlockSpec((1, d), lambda i: (0, 0)),
        ],
        out_specs=pl.BlockSpec((tm, d), lambda i: (i, 0)),
        out_shape=jax.ShapeDtypeStruct((n, d), F32),
        compiler_params=_params("parallel"),
    )(x, o_a, o_b, w_a, w_b, g)


def _encoder_layer(x, p, bias_tbl):
    b, t, d = x.shape
    n = b * t
    w = p['decay_up_fwd'].shape[1]
    x = x.reshape(n, d)
    x = _ffn(x, p['ffn1_pre_g'], p['ffn1_post_g'], p['ffn1_w_gate'], p['ffn1_w_up'], p['ffn1_w_down'])

    z_rwkv = _norm_proj(x, p['mix_pre_g'], p['w_in_rwkv'], F32)
    qkv = _norm_proj(x, p['mix_pre_g'], p['w_in_nat'], BF16)

    r, k, v, lwf, lwb, af, ab, g = _rwkv_prep(
        z_rwkv, t, p['rwkv_shift_mix'],
        p['decay_bias_fwd'], p['decay_up_fwd'], p['decay_bias_bwd'], p['decay_up_bwd'],
        p['iclr_bias_fwd'], p['iclr_up_fwd'], p['iclr_bias_bwd'], p['iclr_up_bwd'], p['gate_up'])
    seq = lambda a: a.reshape(b, t, w)
    yf = _wkv(seq(r), seq(k), seq(v), seq(lwf), seq(af), p['key_norm_scale'], p['key_iclr_mix'], False)
    yb = _wkv(seq(r), seq(k), seq(v), seq(lwb), seq(ab), p['key_norm_scale'], p['key_iclr_mix'], True)
    o_rwkv = _rwkv_post(yf.reshape(n, w), yb.reshape(n, w), r, k, v, af, ab, g,
                        p['key_iclr_mix'], p['bonus_scale'], p['gn_w'], p['gn_b'])

    o_nat = _natten(qkv.reshape(b, t, -1), bias_tbl).reshape(n, -1)

    x = _out_proj(x, o_rwkv, o_nat, p['w_out_rwkv'], p['w_out_nat'], p['mix_post_g'])
    x = _ffn(x, p['ffn2_pre_g'], p['ffn2_post_g'], p['ffn2_w_gate'], p['ffn2_w_up'], p['ffn2_w_down'])
    return x.reshape(b, t, d)


_ROW_PARAMS = ('ffn1_pre_g', 'ffn1_post_g', 'mix_pre_g', 'rwkv_shift_mix',
               'decay_bias_fwd', 'decay_bias_bwd', 'iclr_bias_fwd', 'iclr_bias_bwd',
               'key_norm_scale', 'key_iclr_mix', 'bonus_scale', 'gn_w', 'gn_b',
               'mix_post_g', 'ffn2_pre_g', 'ffn2_post_g')
_F32_MATS = ('decay_up_fwd', 'decay_up_bwd', 'iclr_up_fwd', 'iclr_up_bwd', 'gate_up')
_BF16_MATS = ('ffn1_w_gate', 'ffn1_w_up', 'ffn1_w_down', 'ffn2_w_gate', 'ffn2_w_up', 'ffn2_w_down')


def _layer_params(weights, l):
    p = {name: weights[name][l].reshape(1, -1) for name in _ROW_PARAMS}
    p.update({name: weights[name][l] for name in _F32_MATS})
    p.update({name: weights[name][l].astype(BF16) for name in _BF16_MATS})
    w = weights['decay_up_fwd'].shape[-1]
    rwkv_cols = 3 * w + DECAY_RANK + ICLR_RANK + GATE_RANK
    w_in = weights['w_in'][l].astype(BF16)
    p['w_in_rwkv'] = w_in[:, :rwkv_cols]
    p['w_in_nat'] = w_in[:, rwkv_cols:]
    w_out = weights['w_out'][l].astype(BF16)
    p['w_out_rwkv'] = w_out[:w]
    p['w_out_nat'] = w_out[w:]
    return p, _nat_bias_table(weights['nat_rpb'][l])


def kernel(x_prompt, x_sample, ffn1_pre_g, ffn1_post_g, ffn1_w_gate, ffn1_w_up, ffn1_w_down, mix_pre_g, w_in, rwkv_shift_mix, decay_bias_fwd, decay_up_fwd, decay_bias_bwd, decay_up_bwd, iclr_bias_fwd, iclr_up_fwd, iclr_bias_bwd, iclr_up_bwd, gate_up, key_norm_scale, key_iclr_mix, bonus_scale, gn_w, gn_b, nat_rpb, w_out, mix_post_g, ffn2_pre_g, ffn2_post_g, ffn2_w_gate, ffn2_w_up, ffn2_w_down):
    weights = dict(
        ffn1_pre_g=ffn1_pre_g, ffn1_post_g=ffn1_post_g, ffn1_w_gate=ffn1_w_gate, ffn1_w_up=ffn1_w_up,
        ffn1_w_down=ffn1_w_down, mix_pre_g=mix_pre_g, w_in=w_in, rwkv_shift_mix=rwkv_shift_mix,
        decay_bias_fwd=decay_bias_fwd, decay_up_fwd=decay_up_fwd, decay_bias_bwd=decay_bias_bwd,
        decay_up_bwd=decay_up_bwd, iclr_bias_fwd=iclr_bias_fwd, iclr_up_fwd=iclr_up_fwd,
        iclr_bias_bwd=iclr_bias_bwd, iclr_up_bwd=iclr_up_bwd, gate_up=gate_up,
        key_norm_scale=key_norm_scale, key_iclr_mix=key_iclr_mix, bonus_scale=bonus_scale,
        gn_w=gn_w, gn_b=gn_b, nat_rpb=nat_rpb, w_out=w_out, mix_post_g=mix_post_g,
        ffn2_pre_g=ffn2_pre_g, ffn2_post_g=ffn2_post_g, ffn2_w_gate=ffn2_w_gate, ffn2_w_up=ffn2_w_up,
        ffn2_w_down=ffn2_w_down)
    depth = w_in.shape[0]
    layers = [_layer_params(weights, l) for l in range(depth)]
    outs = []
    for x in (x_prompt, x_sample):
        for p, bias_tbl in layers:
            x = _encoder_layer(x, p, bias_tbl)
        outs.append(x)
    return tuple(outs)
```

```python
import functools
import math

import jax
import jax.numpy as jnp
from jax import lax
from jax.experimental import pallas as pl
from jax.experimental.pallas import tpu as pltpu

F32 = jnp.float32
BF16 = jnp.bfloat16
HIGHEST = lax.Precision.HIGHEST

HEAD_DIM = 64
GRID_W = 64
WIN_H = 8
WIN_W = 16
DECAY_RANK = 64
ICLR_RANK = 64
GATE_RANK = 128
RMS_EPS = 1e-6
GN_EPS = 64e-5
DECAY_OFFSET = 0.5
FFN_RESIDUAL = 0.5
MASK_VALUE = -1e30

LANES = 128
SUBLANES = 8
VMEM_LIMIT_BYTES = 56 * 1024 * 1024

PAIR = 2 * HEAD_DIM
CHUNK = 64
STACK = 2 * CHUNK


def _params(*semantics):
    return pltpu.CompilerParams(dimension_semantics=semantics, vmem_limit_bytes=VMEM_LIMIT_BYTES)


def _dot(a, b, precision=None):
    return jnp.dot(a, b, preferred_element_type=F32, precision=precision)


def _dot_nt(a, b):
    return lax.dot_general(a, b, (((1,), (1,)), ((), ())), preferred_element_type=F32)


def _dot_tn(a, b):
    return lax.dot_general(a, b, (((0,), (0,)), ((), ())), preferred_element_type=F32)


def _rms(x, g):
    return x * lax.rsqrt(jnp.mean(x * x, axis=-1, keepdims=True) + RMS_EPS) * g


def _pick_tile(n, target):
    t = min(n, target)
    while n % t:
        t //= 2
    return t


def _ffn_kernel(x_ref, gpre_ref, gpost_ref, wg_ref, wu_ref, wd_ref, o_ref, h_ref, acc_ref):
    j = pl.program_id(1)

    @pl.when(j == 0)
    def _():
        h_ref[...] = _rms(x_ref[...], gpre_ref[...]).astype(BF16)
        acc_ref[...] = jnp.zeros_like(acc_ref)

    h = h_ref[...]
    g = _dot(h, wg_ref[...])
    u = _dot(h, wu_ref[...])
    act = (g * jax.nn.sigmoid(g) * u).astype(BF16)
    acc_ref[...] += _dot(act, wd_ref[...])

    @pl.when(j == pl.num_programs(1) - 1)
    def _():
        o_ref[...] = x_ref[...] + FFN_RESIDUAL * _rms(acc_ref[...], gpost_ref[...])


def _ffn(x, g_pre, g_post, w_gate, w_up, w_down):
    n, d = x.shape
    f = w_gate.shape[1]
    tm = _pick_tile(n, 512)
    tf = _pick_tile(f, 512)
    return pl.pallas_call(
        _ffn_kernel,
        grid=(n // tm, f // tf),
        in_specs=[
            pl.BlockSpec((tm, d), lambda i, j: (i, 0)),
            pl.BlockSpec((1, d), lambda i, j: (0, 0)),
            pl.BlockSpec((1, d), lambda i, j: (0, 0)),
            pl.BlockSpec((d, tf), lambda i, j: (0, j)),
            pl.BlockSpec((d, tf), lambda i, j: (0, j)),
            pl.BlockSpec((tf, d), lambda i, j: (j, 0)),
        ],
        out_specs=pl.BlockSpec((tm, d), lambda i, j: (i, 0)),
        out_shape=jax.ShapeDtypeStruct((n, d), F32),
        scratch_shapes=[pltpu.VMEM((tm, d), BF16), pltpu.VMEM((tm, d), F32)],
        compiler_params=_params("parallel", "arbitrary"),
    )(x, g_pre, g_post, w_gate, w_up, w_down)


def _norm_proj_kernel(x_ref, g_ref, w_ref, o_ref):
    h = _rms(x_ref[...], g_ref[...]).astype(BF16)
    o_ref[...] = _dot(h, w_ref[...]).astype(o_ref.dtype)


def _norm_proj(x, g, w, out_dtype):
    n, d = x.shape
    c = w.shape[1]
    tm = _pick_tile(n, 256)
    return pl.pallas_call(
        _norm_proj_kernel,
        grid=(n // tm,),
        in_specs=[
            pl.BlockSpec((tm, d), lambda i: (i, 0)),
            pl.BlockSpec((1, d), lambda i: (0, 0)),
            pl.BlockSpec((d, c), lambda i: (0, 0)),
        ],
        out_specs=pl.BlockSpec((tm, c), lambda i: (i, 0)),
        out_shape=jax.ShapeDtypeStruct((n, c), out_dtype),
        compiler_params=_params("parallel"),
    )(x, g, w)


def _rwkv_prep_kernel(seq_len, z_ref, zp_ref, zn_ref, mu_ref,
                      dbf_ref, duf_ref, dbb_ref, dub_ref, ibf_ref, iuf_ref, ibb_ref, iub_ref, gu_ref,
                      r_ref, k_ref, v_ref, lwf_ref, lwb_ref, af_ref, ab_ref, g_ref, u_ref):
    i = pl.program_id(0)
    tm = z_ref.shape[0]
    w = r_ref.shape[1]
    z = z_ref[...]
    row = lax.broadcasted_iota(jnp.int32, (tm, 1), 0)
    at_seq_start = (i * tm) % seq_len == 0
    at_seq_end = ((i + 1) * tm) % seq_len == 0
    prev_row = jnp.where(at_seq_start, 0.0, zp_ref[SUBLANES - 1:SUBLANES, :])
    next_row = jnp.where(at_seq_end, 0.0, zn_ref[0:1, :])
    prev = jnp.where(row == 0, prev_row, pltpu.roll(z, 1, axis=0))
    nxt = jnp.where(row == tm - 1, next_row, pltpu.roll(z, tm - 1, axis=0))
    u_ref[...] = z + mu_ref[...] * (0.5 * (prev + nxt) - z)

    r_ref[...] = u_ref[:, 0:w]
    k_ref[...] = u_ref[:, w:2 * w]
    v_ref[...] = u_ref[:, 2 * w:3 * w]
    o = 3 * w
    dec_lo = jnp.tanh(u_ref[:, o:o + DECAY_RANK])
    o += DECAY_RANK
    iclr_lo = u_ref[:, o:o + ICLR_RANK]
    o += ICLR_RANK
    gate_lo = jax.nn.sigmoid(u_ref[:, o:o + GATE_RANK])

    scale = -math.exp(-DECAY_OFFSET)
    lwf_ref[...] = scale * jax.nn.sigmoid(dbf_ref[...] + _dot(dec_lo, duf_ref[...], HIGHEST))
    lwb_ref[...] = scale * jax.nn.sigmoid(dbb_ref[...] + _dot(dec_lo, dub_ref[...], HIGHEST))
    af_ref[...] = jax.nn.sigmoid(ibf_ref[...] + _dot(iclr_lo, iuf_ref[...], HIGHEST))
    ab_ref[...] = jax.nn.sigmoid(ibb_ref[...] + _dot(iclr_lo, iub_ref[...], HIGHEST))
    g_ref[...] = _dot(gate_lo, gu_ref[...], HIGHEST)


def _rwkv_prep(z, seq_len, mu, dbf, duf, dbb, dub, ibf, iuf, ibb, iub, gate_up):
    n, c = z.shape
    w = duf.shape[1]
    tm = _pick_tile(seq_len, 256)
    hb = tm // SUBLANES
    last_hb = n // SUBLANES - 1
    row_spec = lambda cols: pl.BlockSpec((1, cols), lambda i: (0, 0))
    mat_spec = lambda rows: pl.BlockSpec((rows, w), lambda i: (0, 0))
    out_spec = pl.BlockSpec((tm, w), lambda i: (i, 0))
    out = jax.ShapeDtypeStruct((n, w), F32)
    return pl.pallas_call(
        functools.partial(_rwkv_prep_kernel, seq_len),
        grid=(n // tm,),
        in_specs=[
            pl.BlockSpec((tm, c), lambda i: (i, 0)),
            pl.BlockSpec((SUBLANES, c), lambda i: (jnp.maximum(i * hb - 1, 0), 0)),
            pl.BlockSpec((SUBLANES, c), lambda i: (jnp.minimum((i + 1) * hb, last_hb), 0)),
            row_spec(c),
            row_spec(w), mat_spec(DECAY_RANK), row_spec(w), mat_spec(DECAY_RANK),
            row_spec(w), mat_spec(ICLR_RANK), row_spec(w), mat_spec(ICLR_RANK),
            mat_spec(GATE_RANK),
        ],
        out_specs=[out_spec] * 8,
        out_shape=[out] * 8,
        scratch_shapes=[pltpu.VMEM((tm, c), F32)],
        compiler_params=_params("parallel"),
    )(z, z, z, mu, dbf, duf, dbb, dub, ibf, iuf, ibb, iub, gate_up)


def _stack(x, lane_lo):
    return jnp.concatenate([jnp.where(lane_lo, x, 0.0), jnp.where(lane_lo, 0.0, x)], axis=0)


def _wkv_kernel(reverse, r_ref, k_ref, v_ref, lw_ref, al_ref, kns_ref, mix_ref, y_ref, s_ref):
    tc = pl.program_id(1)
    n_sub = r_ref.shape[0] // CHUNK
    n_pair = r_ref.shape[1] // PAIR

    @pl.when(tc == 0)
    def _():
        s_ref[...] = jnp.zeros_like(s_ref)

    ii = lax.broadcasted_iota(jnp.int32, (STACK, STACK), 0)
    jj = lax.broadcasted_iota(jnp.int32, (STACK, STACK), 1)
    ti = ii % CHUNK
    tj = jj % CHUNK
    same_head = (ii // CHUNK) == (jj // CHUNK)
    ii2 = lax.broadcasted_iota(jnp.int32, (STACK, 2 * STACK), 0)
    jj2 = lax.broadcasted_iota(jnp.int32, (STACK, 2 * STACK), 1)
    same_head2 = (ii2 // CHUNK) == ((jj2 % STACK) // CHUNK)
    if reverse:
        strict = same_head & (ti < tj)
        incl2 = same_head2 & (ii2 % CHUNK <= jj2 % CHUNK)
    else:
        strict = same_head & (ti > tj)
        incl2 = same_head2 & (ii2 % CHUNK >= jj2 % CHUNK)
    ci = lax.broadcasted_iota(jnp.int32, (CHUNK, CHUNK), 0)
    cj = lax.broadcasted_iota(jnp.int32, (CHUNK, CHUNK), 1)
    cum_tri = jnp.where((ci <= cj) if reverse else (ci >= cj), 1.0, 0.0).astype(F32)
    eye = jnp.where(ii == jj, 1.0, 0.0).astype(F32)
    head_ones = jnp.where(
        (lax.broadcasted_iota(jnp.int32, (PAIR, PAIR), 0) // HEAD_DIM)
        == (lax.broadcasted_iota(jnp.int32, (PAIR, PAIR), 1) // HEAD_DIM), 1.0, 0.0).astype(F32)
    lane_lo = lax.broadcasted_iota(jnp.int32, (CHUNK, PAIR), 1) < HEAD_DIM
    end_row = 0 if reverse else CHUNK - 1

    level_masks = []
    m = 1
    while m < CHUNK:
        level_masks.append(strict & (ti // (2 * m) == tj // (2 * m)) & (ti // m != tj // m))
        m *= 2

    def sub_chunk(step, carry):
        c = (n_sub - 1 - step) if reverse else step
        rows = pl.ds(pl.multiple_of(c * CHUNK, CHUNK), CHUNK)
        pairs = range(n_pair)
        lanes = [slice(p * PAIR, (p + 1) * PAIR) for p in pairs]
        ops = []
        for p in pairs:
            r = r_ref[rows, lanes[p]]
            k = k_ref[rows, lanes[p]]
            v = v_ref[rows, lanes[p]]
            lw = lw_ref[rows, lanes[p]]
            al = al_ref[rows, lanes[p]]

            kn = k * kns_ref[:, lanes[p]]
            norm = jnp.sqrt(_dot(kn * kn, head_ones, HIGHEST))
            kk = kn / jnp.maximum(norm, 1e-12)
            kd = k * (1.0 + (al - 1.0) * mix_ref[:, lanes[p]])
            b = kk * al

            cum = _dot(cum_tri, lw, HIGHEST)
            tot = cum[end_row:end_row + 1, :]
            e_in = jnp.exp(cum)
            e_ex = jnp.exp(cum - lw)
            e_inv = jnp.exp(-cum)
            e_end = jnp.exp(tot - cum)
            ops.append(dict(
                rt=_stack(r * e_in, lane_lo).astype(BF16),
                at=_stack(-kk * e_ex, lane_lo).astype(BF16),
                bt=_stack(b * e_inv, lane_lo).astype(BF16),
                kt=_stack(kd * e_inv, lane_lo).astype(BF16),
                bh=_stack(b * e_end, lane_lo).astype(BF16),
                kh=_stack(kd * e_end, lane_lo).astype(BF16),
                vs=_stack(v, lane_lo).astype(BF16),
                decay=jnp.exp(tot)))

        m1 = [_dot_nt(jnp.concatenate([o['at'], o['rt']], axis=0),
                      jnp.concatenate([o['bt'], o['kt']], axis=0)) for o in ops]
        l_ab = [m[:STACK, :STACK] for m in m1]
        l_ak = [jnp.where(strict, m[:STACK, STACK:], 0.0).astype(BF16) for m in m1]
        a_r = [jnp.where(incl2, m[STACK:, :], 0.0).astype(BF16) for m in m1]

        inv = [eye + jnp.where(level_masks[0], l, 0.0) for l in l_ab]
        for mask in level_masks[1:]:
            inv_b = [x.astype(BF16) for x in inv]
            t = [_dot(jnp.where(mask, l, 0.0).astype(BF16), x).astype(BF16) for l, x in zip(l_ab, inv_b)]
            inv = [x + _dot(xb, y) for x, xb, y in zip(inv, inv_b, t)]
        inv_b = [x.astype(BF16) for x in inv]

        lakv = [_dot(l, o['vs']).astype(BF16) for l, o in zip(l_ak, ops)]
        wu = [_dot(x, jnp.concatenate([o['at'], y], axis=1)) for x, o, y in zip(inv_b, ops, lakv)]

        s = [s_ref[p] for p in pairs]
        ur = [_dot_nt(jnp.concatenate([w[:, :PAIR].astype(BF16), o['rt']], axis=0), x.astype(BF16))
              for w, o, x in zip(wu, ops, s)]
        uv = [jnp.concatenate([(x[:STACK] + w[:, PAIR:]).astype(BF16), o['vs']], axis=0)
              for x, w, o in zip(ur, wu, ops)]
        for p in pairs:
            s_ref[p] = s[p] * ops[p]['decay'] + _dot_tn(
                uv[p], jnp.concatenate([ops[p]['bh'], ops[p]['kh']], axis=0))
        for p in pairs:
            ys = ur[p][STACK:] + _dot(a_r[p], uv[p])
            y_ref[rows, lanes[p]] = ys[:CHUNK] + ys[CHUNK:]
        return carry

    lax.fori_loop(0, n_sub, sub_chunk, 0)


def _wkv(r, k, v, lw, al, kns, mix, reverse):
    b, t, w = r.shape
    lc = _pick_tile(t, 256)
    n_t = t // lc
    tmap = (lambda bi, ti: (bi, n_t - 1 - ti, 0)) if reverse else (lambda bi, ti: (bi, ti, 0))
    seq_spec = pl.BlockSpec((None, lc, w), tmap)
    row_spec = pl.BlockSpec((1, w), lambda bi, ti: (0, 0))
    return pl.pallas_call(
        functools.partial(_wkv_kernel, reverse),
        grid=(b, n_t),
        in_specs=[seq_spec] * 5 + [row_spec] * 2,
        out_specs=seq_spec,
        out_shape=jax.ShapeDtypeStruct((b, t, w), F32),
        scratch_shapes=[pltpu.VMEM((w // PAIR, PAIR, PAIR), F32)],
        compiler_params=_params("parallel", "arbitrary"),
    )(r, k, v, lw, al, kns, mix)


def _rwkv_post_kernel(yf_ref, yb_ref, r_ref, k_ref, v_ref, af_ref, ab_ref, g_ref,
                      mix_ref, bonus_ref, gnw_ref, gnb_ref, o_ref):
    n_pair = o_ref.shape[1] // PAIR
    head_ones = jnp.where(
        (lax.broadcasted_iota(jnp.int32, (PAIR, PAIR), 0) // HEAD_DIM)
        == (lax.broadcasted_iota(jnp.int32, (PAIR, PAIR), 1) // HEAD_DIM), 1.0, 0.0).astype(F32)
    for p in range(n_pair):
        lanes = slice(p * PAIR, (p + 1) * PAIR)
        y = yf_ref[:, lanes] + yb_ref[:, lanes]
        mean = _dot(y, head_ones, HIGHEST) * (1.0 / HEAD_DIM)
        yc = y - mean
        var = _dot(yc * yc, head_ones, HIGHEST) * (1.0 / HEAD_DIM)
        yn = yc * lax.rsqrt(var + GN_EPS) * gnw_ref[:, lanes] + gnb_ref[:, lanes]
        k = k_ref[:, lanes]
        mix = mix_ref[:, lanes]
        k_f = k * (1.0 + (af_ref[:, lanes] - 1.0) * mix)
        k_b = k * (1.0 + (ab_ref[:, lanes] - 1.0) * mix)
        rk = r_ref[:, lanes] * (0.5 * (k_f + k_b)) * bonus_ref[:, lanes]
        bonus = _dot(rk, head_ones, HIGHEST) * v_ref[:, lanes]
        o_ref[:, lanes] = ((yn + bonus) * g_ref[:, lanes]).astype(o_ref.dtype)


def _rwkv_post(yf, yb, r, k, v, af, ab, g, mix, bonus_scale, gn_w, gn_b):
    n, w = yf.shape
    tm = _pick_tile(n, 256)
    tile = pl.BlockSpec((tm, w), lambda i: (i, 0))
    row = pl.BlockSpec((1, w), lambda i: (0, 0))
    return pl.pallas_call(
        _rwkv_post_kernel,
        grid=(n // tm,),
        in_specs=[tile] * 8 + [row] * 4,
        out_specs=tile,
        out_shape=jax.ShapeDtypeStruct((n, w), BF16),
        compiler_params=_params("parallel"),
    )(yf, yb, r, k, v, af, ab, g, mix, bonus_scale, gn_w, gn_b)


def _nat_bias_table(rpb):
    off = jnp.arange(WIN_H)
    row_rel = off[None, :] - off[:, None] + (WIN_H - 1)
    cols = jnp.arange(GRID_W)
    col_start = jnp.clip(cols - WIN_W // 2, 0, GRID_W - WIN_W)
    in_win = (cols[None, :] >= col_start[:, None]) & (cols[None, :] < col_start[:, None] + WIN_W)
    col_rel = cols[None, :] - cols[:, None] + (WIN_W - 1)
    pick = (in_win[:, :, None] & (col_rel[:, :, None] == jnp.arange(2 * WIN_W - 1))).astype(F32)
    tbl = jnp.einsum('hdar,jcr->hdjac', rpb.astype(F32)[:, row_rel], pick, precision=HIGHEST)
    tbl = jnp.where(in_win[None, None, :, None, :], tbl, MASK_VALUE)
    return tbl.reshape(rpb.shape[0], WIN_H, GRID_W, WIN_H * GRID_W)


def _nat_row_start(i, n_rows):
    return jnp.clip(i - WIN_H // 2, 0, n_rows - WIN_H)


def _natten_kernel(n_rows, q_ref, k_ref, v_ref, bias_ref, o_ref):
    i = pl.program_id(1)
    n_pair = q_ref.shape[1] // PAIR
    n_keys = WIN_H * GRID_W
    key_rows = pl.ds(pl.multiple_of(_nat_row_start(i, n_rows) * GRID_W, GRID_W), n_keys)
    lane_lo = lax.broadcasted_iota(jnp.int32, (GRID_W, PAIR), 1) < HEAD_DIM
    scale = HEAD_DIM ** -0.5
    for p in range(n_pair):
        lanes = slice(p * PAIR, (p + 1) * PAIR)
        q = q_ref[:, lanes] * scale
        k = k_ref[key_rows, lanes]
        v = v_ref[key_rows, lanes]
        outs = []
        for half in range(2):
            q_h = jnp.where(lane_lo if half == 0 else ~lane_lo, q, jnp.zeros_like(q))
            s = _dot_nt(q_h, k) + bias_ref[2 * p + half, 0]
            e = jnp.exp(s - jnp.max(s, axis=-1, keepdims=True))
            denom = jnp.sum(e, axis=-1, keepdims=True)
            outs.append(_dot(e.astype(BF16), v) / denom)
        o_ref[:, lanes] = jnp.where(lane_lo, outs[0], outs[1]).astype(o_ref.dtype)


def _natten(qkv, bias_tbl):
    b, t, w3 = qkv.shape
    w = w3 // 3
    n_rows = t // GRID_W
    n_heads = w // HEAD_DIM
    bias_map = lambda bi, i: (0, i - _nat_row_start(i, n_rows), 0, 0)
    return pl.pallas_call(
        functools.partial(_natten_kernel, n_rows),
        grid=(b, n_rows),
        in_specs=[
            pl.BlockSpec((None, GRID_W, w), lambda bi, i: (bi, i, 0)),
            pl.BlockSpec((None, t, w), lambda bi, i: (bi, 0, 1)),
            pl.BlockSpec((None, t, w), lambda bi, i: (bi, 0, 2)),
            pl.BlockSpec((n_heads, 1, GRID_W, WIN_H * GRID_W), bias_map),
        ],
        out_specs=pl.BlockSpec((None, GRID_W, w), lambda bi, i: (bi, i, 0)),
        out_shape=jax.ShapeDtypeStruct((b, t, w), BF16),
        compiler_params=_params("parallel", "arbitrary"),
    )(qkv, qkv, qkv, bias_tbl)


def _out_proj_kernel(x_ref, oa_ref, ob_ref, wa_ref, wb_ref, g_ref, o_ref):
    o = _dot(oa_ref[...], wa_ref[...]) + _dot(ob_ref[...], wb_ref[...])
    o_ref[...] = x_ref[...] + _rms(o, g_ref[...])


def _out_proj(x, o_a, o_b, w_a, w_b, g):
    n, d = x.shape
    w = o_a.shape[1]
    tm = _pick_tile(n, 256)
    return pl.pallas_call(
        _out_proj_kernel,
        grid=(n // tm,),
        in_specs=[
            pl.BlockSpec((tm, d), lambda i: (i, 0)),
            pl.BlockSpec((tm, w), lambda i: (i, 0)),
            pl.BlockSpec((tm, w), lambda i: (i, 0)),
            pl.BlockSpec((w, d), lambda i: (0, 0)),
            pl.BlockSpec((w, d), lambda i: (0, 0)),
            pl.BlockSpec((1, d), lambda i: (0, 0)),
        ],
        out_specs=pl.BlockSpec((tm, d), lambda i: (i, 0)),
        out_shape=jax.ShapeDtypeStruct((n, d), F32),
        compiler_params=_params("parallel"),
    )(x, o_a, o_b, w_a, w_b, g)


def _encoder_layer(x, p, bias_tbl):
    b, t, d = x.shape
    n = b * t
    w = p['decay_up_fwd'].shape[1]
    x = x.reshape(n, d)
    x = _ffn(x, p['ffn1_pre_g'], p['ffn1_post_g'], p['ffn1_w_gate'], p['ffn1_w_up'], p['ffn1_w_down'])

    z_rwkv = _norm_proj(x, p['mix_pre_g'], p['w_in_rwkv'], F32)
    qkv = _norm_proj(x, p['mix_pre_g'], p['w_in_nat'], BF16)

    r, k, v, lwf, lwb, af, ab, g = _rwkv_prep(
        z_rwkv, t, p['rwkv_shift_mix'],
        p['decay_bias_fwd'], p['decay_up_fwd'], p['decay_bias_bwd'], p['decay_up_bwd'],
        p['iclr_bias_fwd'], p['iclr_up_fwd'], p['iclr_bias_bwd'], p['iclr_up_bwd'], p['gate_up'])
    seq = lambda a: a.reshape(b, t, w)
    yf = _wkv(seq(r), seq(k), seq(v), seq(lwf), seq(af), p['key_norm_scale'], p['key_iclr_mix'], False)
    yb = _wkv(seq(r), seq(k), seq(v), seq(lwb), seq(ab), p['key_norm_scale'], p['key_iclr_mix'], True)
    o_rwkv = _rwkv_post(yf.reshape(n, w), yb.reshape(n, w), r, k, v, af, ab, g,
                        p['key_iclr_mix'], p['bonus_scale'], p['gn_w'], p['gn_b'])

    o_nat = _natten(qkv.reshape(b, t, -1), bias_tbl).reshape(n, -1)

    x = _out_proj(x, o_rwkv, o_nat, p['w_out_rwkv'], p['w_out_nat'], p['mix_post_g'])
    x = _ffn(x, p['ffn2_pre_g'], p['ffn2_post_g'], p['ffn2_w_gate'], p['ffn2_w_up'], p['ffn2_w_down'])
    return x.reshape(b, t, d)


_ROW_PARAMS = ('ffn1_pre_g', 'ffn1_post_g', 'mix_pre_g', 'rwkv_shift_mix',
               'decay_bias_fwd', 'decay_bias_bwd', 'iclr_bias_fwd', 'iclr_bias_bwd',
               'key_norm_scale', 'key_iclr_mix', 'bonus_scale', 'gn_w', 'gn_b',
               'mix_post_g', 'ffn2_pre_g', 'ffn2_post_g')
_F32_MATS = ('decay_up_fwd', 'decay_up_bwd', 'iclr_up_fwd', 'iclr_up_bwd', 'gate_up')
_BF16_MATS = ('ffn1_w_gate', 'ffn1_w_up', 'ffn1_w_down', 'ffn2_w_gate', 'ffn2_w_up', 'ffn2_w_down')


def _layer_params(weights, l):
    p = {name: weights[name][l].reshape(1, -1) for name in _ROW_PARAMS}
    p.update({name: weights[name][l] for name in _F32_MATS})
    p.update({name: weights[name][l].astype(BF16) for name in _BF16_MATS})
    w = weights['decay_up_fwd'].shape[-1]
    rwkv_cols = 3 * w + DECAY_RANK + ICLR_RANK + GATE_RANK
    w_in = weights['w_in'][l].astype(BF16)
    p['w_in_rwkv'] = w_in[:, :rwkv_cols]
    p['w_in_nat'] = w_in[:, rwkv_cols:]
    w_out = weights['w_out'][l].astype(BF16)
    p['w_out_rwkv'] = w_out[:w]
    p['w_out_nat'] = w_out[w:]
    return p, _nat_bias_table(weights['nat_rpb'][l])


def kernel(x_prompt, x_sample, ffn1_pre_g, ffn1_post_g, ffn1_w_gate, ffn1_w_up, ffn1_w_down, mix_pre_g, w_in, rwkv_shift_mix, decay_bias_fwd, decay_up_fwd, decay_bias_bwd, decay_up_bwd, iclr_bias_fwd, iclr_up_fwd, iclr_bias_bwd, iclr_up_bwd, gate_up, key_norm_scale, key_iclr_mix, bonus_scale, gn_w, gn_b, nat_rpb, w_out, mix_post_g, ffn2_pre_g, ffn2_post_g, ffn2_w_gate, ffn2_w_up, ffn2_w_down):
    weights = dict(
        ffn1_pre_g=ffn1_pre_g, ffn1_post_g=ffn1_post_g, ffn1_w_gate=ffn1_w_gate, ffn1_w_up=ffn1_w_up,
        ffn1_w_down=ffn1_w_down, mix_pre_g=mix_pre_g, w_in=w_in, rwkv_shift_mix=rwkv_shift_mix,
        decay_bias_fwd=decay_bias_fwd, decay_up_fwd=decay_up_fwd, decay_bias_bwd=decay_bias_bwd,
        decay_up_bwd=decay_up_bwd, iclr_bias_fwd=iclr_bias_fwd, iclr_up_fwd=iclr_up_fwd,
        iclr_bias_bwd=iclr_bias_bwd, iclr_up_bwd=iclr_up_bwd, gate_up=gate_up,
        key_norm_scale=key_norm_scale, key_iclr_mix=key_iclr_mix, bonus_scale=bonus_scale,
        gn_w=gn_w, gn_b=gn_b, nat_rpb=nat_rpb, w_out=w_out, mix_post_g=mix_post_g,
        ffn2_pre_g=ffn2_pre_g, ffn2_post_g=ffn2_post_g, ffn2_w_gate=ffn2_w_gate, ffn2_w_up=ffn2_w_up,
        ffn2_w_down=ffn2_w_down)
    depth = w_in.shape[0]
    layers = [_layer_params(weights, l) for l in range(depth)]
    outs = []
    for x in (x_prompt, x_sample):
        for p, bias_tbl in layers:
            x = _encoder_layer(x, p, bias_tbl)
        outs.append(x)
    return tuple(outs)
```

```python
import functools
import math

import jax
import jax.numpy as jnp
from jax import lax
from jax.experimental import pallas as pl
from jax.experimental.pallas import tpu as pltpu

F32 = jnp.float32
BF16 = jnp.bfloat16
HIGHEST = lax.Precision.HIGHEST

HEAD_DIM = 64
GRID_W = 64
WIN_H = 8
WIN_W = 16
DECAY_RANK = 64
ICLR_RANK = 64
GATE_RANK = 128
RMS_EPS = 1e-6
GN_EPS = 64e-5
DECAY_OFFSET = 0.5
FFN_RESIDUAL = 0.5
MASK_VALUE = -1e30

LANES = 128
SUBLANES = 8
VMEM_LIMIT_BYTES = 56 * 1024 * 1024

PAIR = 2 * HEAD_DIM
CHUNK = 64
STACK = 2 * CHUNK


def _params(*semantics):
    return pltpu.CompilerParams(dimension_semantics=semantics, vmem_limit_bytes=VMEM_LIMIT_BYTES)


def _dot(a, b, precision=None):
    return jnp.dot(a, b, preferred_element_type=F32, precision=precision)


def _dot_nt(a, b):
    return lax.dot_general(a, b, (((1,), (1,)), ((), ())), preferred_element_type=F32)


def _dot_tn(a, b):
    return lax.dot_general(a, b, (((0,), (0,)), ((), ())), preferred_element_type=F32)


def _rms(x, g):
    return x * lax.rsqrt(jnp.mean(x * x, axis=-1, keepdims=True) + RMS_EPS) * g


def _head_ones():
    return jnp.where(
        (lax.broadcasted_iota(jnp.int32, (PAIR, PAIR), 0) // HEAD_DIM)
        == (lax.broadcasted_iota(jnp.int32, (PAIR, PAIR), 1) // HEAD_DIM), 1.0, 0.0).astype(BF16)


def _split_dot(x, w, terms, lhs=False):
    acc = None
    rem = x
    for t in range(terms):
        piece = rem.astype(BF16)
        d = _dot(w, piece) if lhs else _dot(piece, w)
        acc = d if acc is None else acc + d
        if t + 1 < terms:
            rem = rem - piece.astype(F32)
    return acc


def _pick_tile(n, target):
    t = min(n, target)
    while n % t:
        t //= 2
    return t


def _ffn_kernel(x_ref, gpre_ref, gpost_ref, wg_ref, wu_ref, wd_ref, o_ref, h_ref, acc_ref):
    j = pl.program_id(1)

    @pl.when(j == 0)
    def _():
        h_ref[...] = _rms(x_ref[...], gpre_ref[...]).astype(BF16)
        acc_ref[...] = jnp.zeros_like(acc_ref)

    h = h_ref[...]
    g = _dot(h, wg_ref[...])
    u = _dot(h, wu_ref[...])
    act = (g * jax.nn.sigmoid(g) * u).astype(BF16)
    acc_ref[...] += _dot(act, wd_ref[...])

    @pl.when(j == pl.num_programs(1) - 1)
    def _():
        o_ref[...] = x_ref[...] + FFN_RESIDUAL * _rms(acc_ref[...], gpost_ref[...])


def _ffn(x, g_pre, g_post, w_gate, w_up, w_down):
    n, d = x.shape
    f = w_gate.shape[1]
    tm = _pick_tile(n, 512)
    tf = _pick_tile(f, 512)
    return pl.pallas_call(
        _ffn_kernel,
        grid=(n // tm, f // tf),
        in_specs=[
            pl.BlockSpec((tm, d), lambda i, j: (i, 0)),
            pl.BlockSpec((1, d), lambda i, j: (0, 0)),
            pl.BlockSpec((1, d), lambda i, j: (0, 0)),
            pl.BlockSpec((d, tf), lambda i, j: (0, j)),
            pl.BlockSpec((d, tf), lambda i, j: (0, j)),
            pl.BlockSpec((tf, d), lambda i, j: (j, 0)),
        ],
        out_specs=pl.BlockSpec((tm, d), lambda i, j: (i, 0)),
        out_shape=jax.ShapeDtypeStruct((n, d), F32),
        scratch_shapes=[pltpu.VMEM((tm, d), BF16), pltpu.VMEM((tm, d), F32)],
        compiler_params=_params("parallel", "arbitrary"),
    )(x, g_pre, g_post, w_gate, w_up, w_down)


def _norm_proj_kernel(x_ref, g_ref, w_ref, o_ref):
    h = _rms(x_ref[...], g_ref[...]).astype(BF16)
    o_ref[...] = _dot(h, w_ref[...]).astype(o_ref.dtype)


def _norm_proj(x, g, w, out_dtype):
    n, d = x.shape
    c = w.shape[1]
    tm = _pick_tile(n, 256)
    return pl.pallas_call(
        _norm_proj_kernel,
        grid=(n // tm,),
        in_specs=[
            pl.BlockSpec((tm, d), lambda i: (i, 0)),
            pl.BlockSpec((1, d), lambda i: (0, 0)),
            pl.BlockSpec((d, c), lambda i: (0, 0)),
        ],
        out_specs=pl.BlockSpec((tm, c), lambda i: (i, 0)),
        out_shape=jax.ShapeDtypeStruct((n, c), out_dtype),
        compiler_params=_params("parallel"),
    )(x, g, w)


def _rwkv_prep_kernel(seq_len, z_ref, zp_ref, zn_ref, mu_ref,
                      dbf_ref, duf_ref, dbb_ref, dub_ref, ibf_ref, iuf_ref, ibb_ref, iub_ref, gu_ref,
                      kns_ref, mix_ref,
                      r_ref, v_ref, kk_ref, kdf_ref, kdb_ref, bf_ref, bb_ref, lwf_ref, lwb_ref, g_ref, u_ref):
    i = pl.program_id(0)
    tm = z_ref.shape[0]
    w = r_ref.shape[1]
    z = z_ref[...]
    row = lax.broadcasted_iota(jnp.int32, (tm, 1), 0)
    at_seq_start = (i * tm) % seq_len == 0
    at_seq_end = ((i + 1) * tm) % seq_len == 0
    prev_row = jnp.where(at_seq_start, 0.0, zp_ref[SUBLANES - 1:SUBLANES, :])
    next_row = jnp.where(at_seq_end, 0.0, zn_ref[0:1, :])
    prev = jnp.where(row == 0, prev_row, pltpu.roll(z, 1, axis=0))
    nxt = jnp.where(row == tm - 1, next_row, pltpu.roll(z, tm - 1, axis=0))
    u_ref[...] = z + mu_ref[...] * (0.5 * (prev + nxt) - z)

    r_ref[...] = u_ref[:, 0:w]
    v_ref[...] = u_ref[:, 2 * w:3 * w]
    o = 3 * w
    dec_lo = jnp.tanh(u_ref[:, o:o + DECAY_RANK]).astype(BF16)
    o += DECAY_RANK
    iclr_lo = u_ref[:, o:o + ICLR_RANK].astype(BF16)
    o += ICLR_RANK
    gate_lo = jax.nn.sigmoid(u_ref[:, o:o + GATE_RANK]).astype(BF16)

    scale = -math.exp(-DECAY_OFFSET)
    lwf_ref[...] = scale * jax.nn.sigmoid(dbf_ref[...] + _dot(dec_lo, duf_ref[...]))
    lwb_ref[...] = scale * jax.nn.sigmoid(dbb_ref[...] + _dot(dec_lo, dub_ref[...]))
    g_ref[...] = _dot(gate_lo, gu_ref[...])

    head_ones = _head_ones()
    for p in range(w // PAIR):
        lanes = slice(p * PAIR, (p + 1) * PAIR)
        kn = u_ref[:, w + p * PAIR:w + (p + 1) * PAIR] * kns_ref[:, lanes]
        norm = jnp.sqrt(_split_dot(kn * kn, head_ones, 2))
        kk_ref[:, lanes] = kn / jnp.maximum(norm, 1e-12)

    k = u_ref[:, w:2 * w]
    mix = mix_ref[...]
    al_f = jax.nn.sigmoid(ibf_ref[...] + _dot(iclr_lo, iuf_ref[...]))
    kdf_ref[...] = k * (1.0 + (al_f - 1.0) * mix)
    bf_ref[...] = kk_ref[...] * al_f
    al_b = jax.nn.sigmoid(ibb_ref[...] + _dot(iclr_lo, iub_ref[...]))
    kdb_ref[...] = k * (1.0 + (al_b - 1.0) * mix)
    bb_ref[...] = kk_ref[...] * al_b


def _rwkv_prep(z, seq_len, mu, dbf, duf, dbb, dub, ibf, iuf, ibb, iub, gate_up, kns, mix):
    n, c = z.shape
    w = duf.shape[1]
    tm = _pick_tile(seq_len, 256)
    hb = tm // SUBLANES
    last_hb = n // SUBLANES - 1
    row_spec = lambda cols: pl.BlockSpec((1, cols), lambda i: (0, 0))
    mat_spec = lambda rows: pl.BlockSpec((rows, w), lambda i: (0, 0))
    out_spec = pl.BlockSpec((tm, w), lambda i: (i, 0))
    out = jax.ShapeDtypeStruct((n, w), F32)
    return pl.pallas_call(
        functools.partial(_rwkv_prep_kernel, seq_len),
        grid=(n // tm,),
        in_specs=[
            pl.BlockSpec((tm, c), lambda i: (i, 0)),
            pl.BlockSpec((SUBLANES, c), lambda i: (jnp.maximum(i * hb - 1, 0), 0)),
            pl.BlockSpec((SUBLANES, c), lambda i: (jnp.minimum((i + 1) * hb, last_hb), 0)),
            row_spec(c),
            row_spec(w), mat_spec(DECAY_RANK), row_spec(w), mat_spec(DECAY_RANK),
            row_spec(w), mat_spec(ICLR_RANK), row_spec(w), mat_spec(ICLR_RANK),
            mat_spec(GATE_RANK), row_spec(w), row_spec(w),
        ],
        out_specs=[out_spec] * 10,
        out_shape=[out] * 10,
        scratch_shapes=[pltpu.VMEM((tm, c), F32)],
        compiler_params=_params("parallel"),
    )(z, z, z, mu, dbf, duf, dbb, dub, ibf, iuf, ibb, iub, gate_up, kns, mix)


def _stack(x, lane_lo):
    return jnp.concatenate([jnp.where(lane_lo, x, 0.0), jnp.where(lane_lo, 0.0, x)], axis=0)


def _wkv_kernel(reverse, r_ref, v_ref, kk_ref, kd_ref, b_ref, lw_ref, y_ref, s_ref):
    tc = pl.program_id(1)
    n_sub = r_ref.shape[0] // CHUNK
    n_pair = r_ref.shape[1] // PAIR

    @pl.when(tc == 0)
    def _():
        s_ref[...] = jnp.zeros_like(s_ref)

    ii = lax.broadcasted_iota(jnp.int32, (STACK, STACK), 0)
    jj = lax.broadcasted_iota(jnp.int32, (STACK, STACK), 1)
    ti = ii % CHUNK
    tj = jj % CHUNK
    same_head = (ii // CHUNK) == (jj // CHUNK)
    ii2 = lax.broadcasted_iota(jnp.int32, (STACK, 2 * STACK), 0)
    jj2 = lax.broadcasted_iota(jnp.int32, (STACK, 2 * STACK), 1)
    same_head2 = (ii2 // CHUNK) == ((jj2 % STACK) // CHUNK)
    if reverse:
        strict = same_head & (ti < tj)
        incl2 = same_head2 & (ii2 % CHUNK <= jj2 % CHUNK)
    else:
        strict = same_head & (ti > tj)
        incl2 = same_head2 & (ii2 % CHUNK >= jj2 % CHUNK)
    ci = lax.broadcasted_iota(jnp.int32, (CHUNK, CHUNK), 0)
    cj = lax.broadcasted_iota(jnp.int32, (CHUNK, CHUNK), 1)
    cum_tri = jnp.where((ci <= cj) if reverse else (ci >= cj), 1.0, 0.0).astype(BF16)
    eye = jnp.where(ii == jj, 1.0, 0.0).astype(F32)
    lane_lo = lax.broadcasted_iota(jnp.int32, (CHUNK, PAIR), 1) < HEAD_DIM
    end_row = 0 if reverse else CHUNK - 1

    level_masks = []
    m = 1
    while m < CHUNK:
        level_masks.append(strict & (ti // (2 * m) == tj // (2 * m)) & (ti // m != tj // m))
        m *= 2

    def sub_chunk(step, carry):
        c = (n_sub - 1 - step) if reverse else step
        rows = pl.ds(pl.multiple_of(c * CHUNK, CHUNK), CHUNK)
        pairs = range(n_pair)
        lanes = [slice(p * PAIR, (p + 1) * PAIR) for p in pairs]
        cum_all = _split_dot(lw_ref[rows, :], cum_tri, 3, lhs=True)
        ops = []
        for p in pairs:
            r = r_ref[rows, lanes[p]]
            v = v_ref[rows, lanes[p]]
            kk = kk_ref[rows, lanes[p]]
            kd = kd_ref[rows, lanes[p]]
            b = b_ref[rows, lanes[p]]
            lw = lw_ref[rows, lanes[p]]

            cum = cum_all[:, lanes[p]]
            tot = cum[end_row:end_row + 1, :]
            e_in = jnp.exp(cum)
            e_ex = jnp.exp(cum - lw)
            e_inv = jnp.exp(-cum)
            e_end = jnp.exp(tot - cum)
            ops.append(dict(
                rt=_stack(r * e_in, lane_lo).astype(BF16),
                at=_stack(-kk * e_ex, lane_lo).astype(BF16),
                bt=_stack(b * e_inv, lane_lo).astype(BF16),
                kt=_stack(kd * e_inv, lane_lo).astype(BF16),
                bh=_stack(b * e_end, lane_lo).astype(BF16),
                kh=_stack(kd * e_end, lane_lo).astype(BF16),
                vs=_stack(v, lane_lo).astype(BF16),
                decay=jnp.exp(tot)))

        m1 = [_dot_nt(jnp.concatenate([o['at'], o['rt']], axis=0),
                      jnp.concatenate([o['bt'], o['kt']], axis=0)) for o in ops]
        l_ab = [m[:STACK, :STACK] for m in m1]
        l_ak = [jnp.where(strict, m[:STACK, STACK:], 0.0).astype(BF16) for m in m1]
        a_r = [jnp.where(incl2, m[STACK:, :], 0.0).astype(BF16) for m in m1]

        inv = [eye + jnp.where(level_masks[0], l, 0.0) for l in l_ab]
        for mask in level_masks[1:]:
            inv_b = [x.astype(BF16) for x in inv]
            t = [_dot(jnp.where(mask, l, 0.0).astype(BF16), x).astype(BF16) for l, x in zip(l_ab, inv_b)]
            inv = [x + _dot(xb, y) for x, xb, y in zip(inv, inv_b, t)]
        inv_b = [x.astype(BF16) for x in inv]

        lakv = [_dot(l, o['vs']).astype(BF16) for l, o in zip(l_ak, ops)]
        wu = [_dot(x, jnp.concatenate([o['at'], y], axis=1)) for x, o, y in zip(inv_b, ops, lakv)]

        s = [s_ref[p] for p in pairs]
        ur = [_dot_nt(jnp.concatenate([w[:, :PAIR].astype(BF16), o['rt']], axis=0), x.astype(BF16))
              for w, o, x in zip(wu, ops, s)]
        uv = [jnp.concatenate([(x[:STACK] + w[:, PAIR:]).astype(BF16), o['vs']], axis=0)
              for x, w, o in zip(ur, wu, ops)]
        for p in pairs:
            s_ref[p] = s[p] * ops[p]['decay'] + _dot_tn(
                uv[p], jnp.concatenate([ops[p]['bh'], ops[p]['kh']], axis=0))
        for p in pairs:
            ys = ur[p][STACK:] + _dot(a_r[p], uv[p])
            y_ref[rows, lanes[p]] = ys[:CHUNK] + ys[CHUNK:]
        return carry

    lax.fori_loop(0, n_sub, sub_chunk, 0)


def _wkv(r, v, kk, kd, bb, lw, reverse):
    b, t, w = r.shape
    lc = _pick_tile(t, 256)
    n_t = t // lc
    tmap = (lambda bi, ti: (bi, n_t - 1 - ti, 0)) if reverse else (lambda bi, ti: (bi, ti, 0))
    seq_spec = pl.BlockSpec((None, lc, w), tmap)
    return pl.pallas_call(
        functools.partial(_wkv_kernel, reverse),
        grid=(b, n_t),
        in_specs=[seq_spec] * 6,
        out_specs=seq_spec,
        out_shape=jax.ShapeDtypeStruct((b, t, w), F32),
        scratch_shapes=[pltpu.VMEM((w // PAIR, PAIR, PAIR), F32)],
        compiler_params=_params("parallel", "arbitrary"),
    )(r, v, kk, kd, bb, lw)


def _rwkv_post_kernel(yf_ref, yb_ref, r_ref, v_ref, kdf_ref, kdb_ref, g_ref,
                      bonus_ref, gnw_ref, gnb_ref, o_ref):
    n_pair = o_ref.shape[1] // PAIR
    head_ones = _head_ones()
    pairs = range(n_pair)
    lanes = [slice(p * PAIR, (p + 1) * PAIR) for p in pairs]
    y = [yf_ref[:, l] + yb_ref[:, l] for l in lanes]
    mean = [_split_dot(x, head_ones, 2) * (1.0 / HEAD_DIM) for x in y]
    yc = [x - m for x, m in zip(y, mean)]
    var = [_split_dot(x * x, head_ones, 2) * (1.0 / HEAD_DIM) for x in yc]
    rk = [r_ref[:, l] * (0.5 * (kdf_ref[:, l] + kdb_ref[:, l])) * bonus_ref[:, l] for l in lanes]
    bonus = [_split_dot(x, head_ones, 2) for x in rk]
    for p in pairs:
        l = lanes[p]
        yn = yc[p] * lax.rsqrt(var[p] + GN_EPS) * gnw_ref[:, l] + gnb_ref[:, l]
        o_ref[:, l] = ((yn + bonus[p] * v_ref[:, l]) * g_ref[:, l]).astype(o_ref.dtype)


def _rwkv_post(yf, yb, r, v, kdf, kdb, g, bonus_scale, gn_w, gn_b):
    n, w = yf.shape
    tm = _pick_tile(n, 256)
    tile = pl.BlockSpec((tm, w), lambda i: (i, 0))
    row = pl.BlockSpec((1, w), lambda i: (0, 0))
    return pl.pallas_call(
        _rwkv_post_kernel,
        grid=(n // tm,),
        in_specs=[tile] * 7 + [row] * 3,
        out_specs=tile,
        out_shape=jax.ShapeDtypeStruct((n, w), BF16),
        compiler_params=_params("parallel"),
    )(yf, yb, r, v, kdf, kdb, g, bonus_scale, gn_w, gn_b)


def _nat_bias_table(rpb):
    off = jnp.arange(WIN_H)
    row_rel = off[None, :] - off[:, None] + (WIN_H - 1)
    cols = jnp.arange(GRID_W)
    col_start = jnp.clip(cols - WIN_W // 2, 0, GRID_W - WIN_W)
    in_win = (cols[None, :] >= col_start[:, None]) & (cols[None, :] < col_start[:, None] + WIN_W)
    col_rel = cols[None, :] - cols[:, None] + (WIN_W - 1)
    pick = (in_win[:, :, None] & (col_rel[:, :, None] == jnp.arange(2 * WIN_W - 1))).astype(F32)
    tbl = jnp.einsum('hdar,jcr->hdjac', rpb.astype(F32)[:, row_rel], pick, precision=HIGHEST)
    tbl = jnp.where(in_win[None, None, :, None, :], tbl, MASK_VALUE)
    return tbl.reshape(rpb.shape[0], WIN_H, GRID_W, WIN_H * GRID_W)


def _nat_row_start(i, n_rows):
    return jnp.clip(i - WIN_H // 2, 0, n_rows - WIN_H)


def _natten_kernel(n_rows, q_ref, k_ref, v_ref, bias_ref, o_ref):
    i = pl.program_id(1)
    n_pair = q_ref.shape[1] // PAIR
    n_keys = WIN_H * GRID_W
    key_rows = pl.ds(pl.multiple_of(_nat_row_start(i, n_rows) * GRID_W, GRID_W), n_keys)
    lane_lo = lax.broadcasted_iota(jnp.int32, (GRID_W, PAIR), 1) < HEAD_DIM
    scale = HEAD_DIM ** -0.5
    heads = range(2 * n_pair)
    lanes = [slice((h // 2) * PAIR, (h // 2 + 1) * PAIR) for h in heads]
    s = []
    for h in heads:
        q = q_ref[:, lanes[h]] * scale
        q_h = jnp.where(lane_lo if h % 2 == 0 else ~lane_lo, q, jnp.zeros_like(q))
        s.append(_dot_nt(q_h, k_ref[key_rows, lanes[h]]) + bias_ref[h, 0])
    e = [jnp.exp(x - jnp.max(x, axis=-1, keepdims=True)) for x in s]
    denom = [jnp.sum(x, axis=-1, keepdims=True) for x in e]
    outs = [_dot(e[h].astype(BF16), v_ref[key_rows, lanes[h]]) / denom[h] for h in heads]
    for p in range(n_pair):
        o_ref[:, lanes[2 * p]] = jnp.where(lane_lo, outs[2 * p], outs[2 * p + 1]).astype(o_ref.dtype)


def _natten(qkv, bias_tbl):
    b, t, w3 = qkv.shape
    w = w3 // 3
    n_rows = t // GRID_W
    n_heads = w // HEAD_DIM
    bias_map = lambda bi, i: (0, i - _nat_row_start(i, n_rows), 0, 0)
    return pl.pallas_call(
        functools.partial(_natten_kernel, n_rows),
        grid=(b, n_rows),
        in_specs=[
            pl.BlockSpec((None, GRID_W, w), lambda bi, i: (bi, i, 0)),
            pl.BlockSpec((None, t, w), lambda bi, i: (bi, 0, 1)),
            pl.BlockSpec((None, t, w), lambda bi, i: (bi, 0, 2)),
            pl.BlockSpec((n_heads, 1, GRID_W, WIN_H * GRID_W), bias_map),
        ],
        out_specs=pl.BlockSpec((None, GRID_W, w), lambda bi, i: (bi, i, 0)),
        out_shape=jax.ShapeDtypeStruct((b, t, w), BF16),
        compiler_params=_params("parallel", "arbitrary"),
    )(qkv, qkv, qkv, bias_tbl)


def _out_proj_kernel(x_ref, oa_ref, ob_ref, wa_ref, wb_ref, g_ref, o_ref):
    o = _dot(oa_ref[...], wa_ref[...]) + _dot(ob_ref[...], wb_ref[...])
    o_ref[...] = x_ref[...] + _rms(o, g_ref[...])


def _out_proj(x, o_a, o_b, w_a, w_b, g):
    n, d = x.shape
    w = o_a.shape[1]
    tm = _pick_tile(n, 256)
    return pl.pallas_call(
        _out_proj_kernel,
        grid=(n // tm,),
        in_specs=[
            pl.BlockSpec((tm, d), lambda i: (i, 0)),
            pl.BlockSpec((tm, w), lambda i: (i, 0)),
            pl.BlockSpec((tm, w), lambda i: (i, 0)),
            pl.BlockSpec((w, d), lambda i: (0, 0)),
            pl.BlockSpec((w, d), lambda i: (0, 0)),
            pl.BlockSpec((1, d), lambda i: (0, 0)),
        ],
        out_specs=pl.BlockSpec((tm, d), lambda i: (i, 0)),
        out_shape=jax.ShapeDtypeStruct((n, d), F32),
        compiler_params=_params("parallel"),
    )(x, o_a, o_b, w_a, w_b, g)


def _encoder_layer(x, p, bias_tbl):
    b, t, d = x.shape
    n = b * t
    w = p['decay_up_fwd'].shape[1]
    x = x.reshape(n, d)
    x = _ffn(x, p['ffn1_pre_g'], p['ffn1_post_g'], p['ffn1_w_gate'], p['ffn1_w_up'], p['ffn1_w_down'])

    z_rwkv = _norm_proj(x, p['mix_pre_g'], p['w_in_rwkv'], F32)
    qkv = _norm_proj(x, p['mix_pre_g'], p['w_in_nat'], BF16)

    r, v, kk, kdf, kdb, bf, bb, lwf, lwb, g = _rwkv_prep(
        z_rwkv, t, p['rwkv_shift_mix'],
        p['decay_bias_fwd'], p['decay_up_fwd'], p['decay_bias_bwd'], p['decay_up_bwd'],
        p['iclr_bias_fwd'], p['iclr_up_fwd'], p['iclr_bias_bwd'], p['iclr_up_bwd'], p['gate_up'],
        p['key_norm_scale'], p['key_iclr_mix'])
    seq = lambda a: a.reshape(b, t, w)
    yf = _wkv(seq(r), seq(v), seq(kk), seq(kdf), seq(bf), seq(lwf), False)
    yb = _wkv(seq(r), seq(v), seq(kk), seq(kdb), seq(bb), seq(lwb), True)
    o_rwkv = _rwkv_post(yf.reshape(n, w), yb.reshape(n, w), r, v, kdf, kdb, g,
                        p['bonus_scale'], p['gn_w'], p['gn_b'])

    o_nat = _natten(qkv.reshape(b, t, -1), bias_tbl).reshape(n, -1)

    x = _out_proj(x, o_rwkv, o_nat, p['w_out_rwkv'], p['w_out_nat'], p['mix_post_g'])
    x = _ffn(x, p['ffn2_pre_g'], p['ffn2_post_g'], p['ffn2_w_gate'], p['ffn2_w_up'], p['ffn2_w_down'])
    return x.reshape(b, t, d)


_ROW_PARAMS = ('ffn1_pre_g', 'ffn1_post_g', 'mix_pre_g', 'rwkv_shift_mix',
               'decay_bias_fwd', 'decay_bias_bwd', 'iclr_bias_fwd', 'iclr_bias_bwd',
               'key_norm_scale', 'key_iclr_mix', 'bonus_scale', 'gn_w', 'gn_b',
               'mix_post_g', 'ffn2_pre_g', 'ffn2_post_g')
_BF16_MATS = ('ffn1_w_gate', 'ffn1_w_up', 'ffn1_w_down', 'ffn2_w_gate', 'ffn2_w_up', 'ffn2_w_down',
              'decay_up_fwd', 'decay_up_bwd', 'iclr_up_fwd', 'iclr_up_bwd', 'gate_up')


def _layer_params(weights, l):
    p = {name: weights[name][l].reshape(1, -1) for name in _ROW_PARAMS}
    p.update({name: weights[name][l].astype(BF16) for name in _BF16_MATS})
    w = weights['decay_up_fwd'].shape[-1]
    rwkv_cols = 3 * w + DECAY_RANK + ICLR_RANK + GATE_RANK
    w_in = weights['w_in'][l].astype(BF16)
    p['w_in_rwkv'] = w_in[:, :rwkv_cols]
    p['w_in_nat'] = w_in[:, rwkv_cols:]
    w_out = weights['w_out'][l].astype(BF16)
    p['w_out_rwkv'] = w_out[:w]
    p['w_out_nat'] = w_out[w:]
    return p, _nat_bias_table(weights['nat_rpb'][l])


def kernel(x_prompt, x_sample, ffn1_pre_g, ffn1_post_g, ffn1_w_gate, ffn1_w_up, ffn1_w_down, mix_pre_g, w_in, rwkv_shift_mix, decay_bias_fwd, decay_up_fwd, decay_bias_bwd, decay_up_bwd, iclr_bias_fwd, iclr_up_fwd, iclr_bias_bwd, iclr_up_bwd, gate_up, key_norm_scale, key_iclr_mix, bonus_scale, gn_w, gn_b, nat_rpb, w_out, mix_post_g, ffn2_pre_g, ffn2_post_g, ffn2_w_gate, ffn2_w_up, ffn2_w_down):
    weights = dict(
        ffn1_pre_g=ffn1_pre_g, ffn1_post_g=ffn1_post_g, ffn1_w_gate=ffn1_w_gate, ffn1_w_up=ffn1_w_up,
        ffn1_w_down=ffn1_w_down, mix_pre_g=mix_pre_g, w_in=w_in, rwkv_shift_mix=rwkv_shift_mix,
        decay_bias_fwd=decay_bias_fwd, decay_up_fwd=decay_up_fwd, decay_bias_bwd=decay_bias_bwd,
        decay_up_bwd=decay_up_bwd, iclr_bias_fwd=iclr_bias_fwd, iclr_up_fwd=iclr_up_fwd,
        iclr_bias_bwd=iclr_bias_bwd, iclr_up_bwd=iclr_up_bwd, gate_up=gate_up,
        key_norm_scale=key_norm_scale, key_iclr_mix=key_iclr_mix, bonus_scale=bonus_scale,
        gn_w=gn_w, gn_b=gn_b, nat_rpb=nat_rpb, w_out=w_out, mix_post_g=mix_post_g,
        ffn2_pre_g=ffn2_pre_g, ffn2_post_g=ffn2_post_g, ffn2_w_gate=ffn2_w_gate, ffn2_w_up=ffn2_w_up,
        ffn2_w_down=ffn2_w_down)
    depth = w_in.shape[0]
    layers = [_layer_params(weights, l) for l in range(depth)]
    outs = []
    for x in (x_prompt, x_sample):
        for p, bias_tbl in layers:
            x = _encoder_layer(x, p, bias_tbl)
        outs.append(x)
    return tuple(outs)
```

```python
import functools
import math

import jax
import jax.numpy as jnp
from jax import lax
from jax.experimental import pallas as pl
from jax.experimental.pallas import tpu as pltpu

F32 = jnp.float32
BF16 = jnp.bfloat16
HIGHEST = lax.Precision.HIGHEST

HEAD_DIM = 64
GRID_W = 64
WIN_H = 8
WIN_W = 16
DECAY_RANK = 64
ICLR_RANK = 64
GATE_RANK = 128
RMS_EPS = 1e-6
GN_EPS = 64e-5
DECAY_OFFSET = 0.5
FFN_RESIDUAL = 0.5
MASK_VALUE = -1e30

LANES = 128
HALO_ROWS = 16
EDGE_ROWS = 8
VMEM_LIMIT_BYTES = 56 * 1024 * 1024

PAIR = 2 * HEAD_DIM
CHUNK = 64
STACK = 2 * CHUNK
WKV_BLOCK = 256
WKV_GROUP = 2


def _params(*semantics):
    return pltpu.CompilerParams(dimension_semantics=semantics, vmem_limit_bytes=VMEM_LIMIT_BYTES)


def _dot(a, b, precision=None):
    return jnp.dot(a, b, preferred_element_type=F32, precision=precision)


def _dot_nt(a, b):
    return lax.dot_general(a, b, (((1,), (1,)), ((), ())), preferred_element_type=F32)


def _dot_tn(a, b):
    return lax.dot_general(a, b, (((0,), (0,)), ((), ())), preferred_element_type=F32)


def _rms(x, g):
    return x * lax.rsqrt(jnp.mean(x * x, axis=-1, keepdims=True) + RMS_EPS) * g


def _head_ones():
    return jnp.where(
        (lax.broadcasted_iota(jnp.int32, (PAIR, PAIR), 0) // HEAD_DIM)
        == (lax.broadcasted_iota(jnp.int32, (PAIR, PAIR), 1) // HEAD_DIM), 1.0, 0.0).astype(BF16)


def _split_dot(x, w, terms, lhs=False):
    acc = None
    rem = x
    for t in range(terms):
        piece = rem.astype(BF16)
        d = _dot(w, piece) if lhs else _dot(piece, w)
        acc = d if acc is None else acc + d
        if t + 1 < terms:
            rem = rem - piece.astype(F32)
    return acc


def _pick_tile(n, target):
    t = min(n, target)
    while n % t:
        t //= 2
    return t


def _ffn_kernel(x_ref, gpre_ref, gpost_ref, wg_ref, wu_ref, wd_ref, o_ref, h_ref, acc_ref):
    j = pl.program_id(1)

    @pl.when(j == 0)
    def _():
        h_ref[...] = _rms(x_ref[...], gpre_ref[...]).astype(BF16)
        acc_ref[...] = jnp.zeros_like(acc_ref)

    h = h_ref[...]
    g = _dot(h, wg_ref[...])
    u = _dot(h, wu_ref[...])
    act = (g * jax.nn.sigmoid(g) * u).astype(BF16)
    acc_ref[...] += _dot(act, wd_ref[...])

    @pl.when(j == pl.num_programs(1) - 1)
    def _():
        o_ref[...] = x_ref[...] + FFN_RESIDUAL * _rms(acc_ref[...], gpost_ref[...])


def _ffn(x, g_pre, g_post, w_gate, w_up, w_down):
    n, d = x.shape
    f = w_gate.shape[1]
    tm = _pick_tile(n, 512)
    tf = _pick_tile(f, 512)
    return pl.pallas_call(
        _ffn_kernel,
        grid=(n // tm, f // tf),
        in_specs=[
            pl.BlockSpec((tm, d), lambda i, j: (i, 0)),
            pl.BlockSpec((1, d), lambda i, j: (0, 0)),
            pl.BlockSpec((1, d), lambda i, j: (0, 0)),
            pl.BlockSpec((d, tf), lambda i, j: (0, j)),
            pl.BlockSpec((d, tf), lambda i, j: (0, j)),
            pl.BlockSpec((tf, d), lambda i, j: (j, 0)),
        ],
        out_specs=pl.BlockSpec((tm, d), lambda i, j: (i, 0)),
        out_shape=jax.ShapeDtypeStruct((n, d), F32),
        scratch_shapes=[pltpu.VMEM((tm, d), BF16), pltpu.VMEM((tm, d), F32)],
        compiler_params=_params("parallel", "arbitrary"),
    )(x, g_pre, g_post, w_gate, w_up, w_down)


def _norm_proj_kernel(x_ref, g_ref, w_ref, o_ref):
    h = _rms(x_ref[...], g_ref[...]).astype(BF16)
    o_ref[...] = _dot(h, w_ref[...]).astype(o_ref.dtype)


def _norm_proj(x, g, w, out_dtype):
    n, d = x.shape
    c = w.shape[1]
    tm = _pick_tile(n, 256)
    return pl.pallas_call(
        _norm_proj_kernel,
        grid=(n // tm,),
        in_specs=[
            pl.BlockSpec((tm, d), lambda i: (i, 0)),
            pl.BlockSpec((1, d), lambda i: (0, 0)),
            pl.BlockSpec((d, c), lambda i: (0, 0)),
        ],
        out_specs=pl.BlockSpec((tm, c), lambda i: (i, 0)),
        out_shape=jax.ShapeDtypeStruct((n, c), out_dtype),
        compiler_params=_params("parallel"),
    )(x, g, w)


def _rwkv_prep_kernel(seq_len, z_ref, zp_ref, zn_ref, mu_ref,
                      dbf_ref, duf_ref, dbb_ref, dub_ref, ibf_ref, iuf_ref, ibb_ref, iub_ref, gu_ref,
                      kns_ref, mix_ref,
                      r_ref, v_ref, kk_ref, kdf_ref, kdb_ref, bf_ref, bb_ref, lwf_ref, lwb_ref, g_ref, u_ref):
    i = pl.program_id(0)
    tm = z_ref.shape[0]
    w = r_ref.shape[1]
    ti = lax.broadcasted_iota(jnp.int32, (tm, tm), 0)
    tj = lax.broadcasted_iota(jnp.int32, (tm, tm), 1)
    neighbours = jnp.where((ti == tj + 1) | (ti + 1 == tj), 1.0, 0.0).astype(BF16)
    z = z_ref[...]
    mu = mu_ref[...]
    u_ref[...] = z + mu * (0.5 * _dot(neighbours, z) - z)
    at_seq_start = (i * tm) % seq_len == 0
    at_seq_end = ((i + 1) * tm) % seq_len == 0
    prev_row = jnp.where(at_seq_start, 0.0, zp_ref[HALO_ROWS - 1:HALO_ROWS, :].astype(F32))
    next_row = jnp.where(at_seq_end, 0.0, zn_ref[0:1, :].astype(F32))
    edge = lax.broadcasted_iota(jnp.int32, (EDGE_ROWS, 1), 0)
    u_ref[0:EDGE_ROWS, :] += jnp.where(edge == 0, 0.5 * mu * prev_row, 0.0)
    u_ref[tm - EDGE_ROWS:tm, :] += jnp.where(edge == EDGE_ROWS - 1, 0.5 * mu * next_row, 0.0)

    r_ref[...] = u_ref[:, 0:w].astype(r_ref.dtype)
    v_ref[...] = u_ref[:, 2 * w:3 * w].astype(v_ref.dtype)
    o = 3 * w
    dec_lo = jnp.tanh(u_ref[:, o:o + DECAY_RANK]).astype(BF16)
    o += DECAY_RANK
    iclr_lo = u_ref[:, o:o + ICLR_RANK].astype(BF16)
    o += ICLR_RANK
    gate_lo = jax.nn.sigmoid(u_ref[:, o:o + GATE_RANK]).astype(BF16)

    scale = -math.exp(-DECAY_OFFSET)
    lwf_ref[...] = scale * jax.nn.sigmoid(dbf_ref[...] + _dot(dec_lo, duf_ref[...]))
    lwb_ref[...] = scale * jax.nn.sigmoid(dbb_ref[...] + _dot(dec_lo, dub_ref[...]))
    g_ref[...] = _dot(gate_lo, gu_ref[...]).astype(g_ref.dtype)

    head_ones = _head_ones()
    for p in range(w // PAIR):
        lanes = slice(p * PAIR, (p + 1) * PAIR)
        k = u_ref[:, w + p * PAIR:w + (p + 1) * PAIR]
        kn = k * kns_ref[:, lanes]
        norm = jnp.sqrt(_split_dot(kn * kn, head_ones, 2))
        kk = kn / jnp.maximum(norm, 1e-12)
        kk_ref[:, lanes] = kk.astype(kk_ref.dtype)
        mix = mix_ref[:, lanes]
        for ib_ref, iu_ref, kd_ref, b_ref in ((ibf_ref, iuf_ref, kdf_ref, bf_ref),
                                              (ibb_ref, iub_ref, kdb_ref, bb_ref)):
            al = jax.nn.sigmoid(ib_ref[:, lanes] + _dot(iclr_lo, iu_ref[:, lanes]))
            kd_ref[:, lanes] = (k * (1.0 + (al - 1.0) * mix)).astype(kd_ref.dtype)
            b_ref[:, lanes] = (kk * al).astype(b_ref.dtype)


def _rwkv_prep(z, seq_len, mu, dbf, duf, dbb, dub, ibf, iuf, ibb, iub, gate_up, kns, mix):
    n, c = z.shape
    assert z.dtype == BF16
    w = duf.shape[1]
    tm = _pick_tile(seq_len, 256)
    hb = tm // HALO_ROWS
    last_hb = n // HALO_ROWS - 1
    row_spec = lambda cols: pl.BlockSpec((1, cols), lambda i: (0, 0))
    mat_spec = lambda rows: pl.BlockSpec((rows, w), lambda i: (0, 0))
    out_spec = pl.BlockSpec((tm, w), lambda i: (i, 0))
    outs = [jax.ShapeDtypeStruct((n, w), dt) for dt in (BF16,) * 7 + (F32, F32, BF16)]
    return pl.pallas_call(
        functools.partial(_rwkv_prep_kernel, seq_len),
        grid=(n // tm,),
        in_specs=[
            pl.BlockSpec((tm, c), lambda i: (i, 0)),
            pl.BlockSpec((HALO_ROWS, c), lambda i: (jnp.maximum(i * hb - 1, 0), 0)),
            pl.BlockSpec((HALO_ROWS, c), lambda i: (jnp.minimum((i + 1) * hb, last_hb), 0)),
            row_spec(c),
            row_spec(w), mat_spec(DECAY_RANK), row_spec(w), mat_spec(DECAY_RANK),
            row_spec(w), mat_spec(ICLR_RANK), row_spec(w), mat_spec(ICLR_RANK),
            mat_spec(GATE_RANK), row_spec(w), row_spec(w),
        ],
        out_specs=[out_spec] * 10,
        out_shape=outs,
        scratch_shapes=[pltpu.VMEM((tm, c), F32)],
        compiler_params=_params("parallel"),
    )(z, z, z, mu, dbf, duf, dbb, dub, ibf, iuf, ibb, iub, gate_up, kns, mix)


def _stack(x, lane_lo):
    return jnp.concatenate([jnp.where(lane_lo, x, 0.0), jnp.where(lane_lo, 0.0, x)], axis=0)


def _wkv_kernel(reverse, r_ref, v_ref, kk_ref, kd_ref, b_ref, lw_ref, y_ref,
                s_ref, wr_s, u0_s, ar_s, vs_s, bk_s, dec_s):
    tc = pl.program_id(1)
    n_sub = r_ref.shape[0] // CHUNK
    n_pair = r_ref.shape[1] // PAIR

    @pl.when(tc == 0)
    def _():
        s_ref[...] = jnp.zeros_like(s_ref)

    ii = lax.broadcasted_iota(jnp.int32, (STACK, STACK), 0)
    jj = lax.broadcasted_iota(jnp.int32, (STACK, STACK), 1)
    ti = ii % CHUNK
    tj = jj % CHUNK
    same_head = (ii // CHUNK) == (jj // CHUNK)
    ii2 = lax.broadcasted_iota(jnp.int32, (STACK, 2 * STACK), 0)
    jj2 = lax.broadcasted_iota(jnp.int32, (STACK, 2 * STACK), 1)
    same_head2 = (ii2 // CHUNK) == ((jj2 % STACK) // CHUNK)
    if reverse:
        strict = same_head & (ti < tj)
        incl2 = same_head2 & (ii2 % CHUNK <= jj2 % CHUNK)
    else:
        strict = same_head & (ti > tj)
        incl2 = same_head2 & (ii2 % CHUNK >= jj2 % CHUNK)
    ci = lax.broadcasted_iota(jnp.int32, (CHUNK, CHUNK), 0)
    cj = lax.broadcasted_iota(jnp.int32, (CHUNK, CHUNK), 1)
    cum_tri = jnp.where((ci <= cj) if reverse else (ci >= cj), 1.0, 0.0).astype(BF16)
    eye = jnp.where(ii == jj, 1.0, 0.0).astype(F32)
    lane_lo = lax.broadcasted_iota(jnp.int32, (CHUNK, PAIR), 1) < HEAD_DIM
    end_row = 0 if reverse else CHUNK - 1

    level_masks = []
    m = 1
    while m < CHUNK:
        level_masks.append(strict & (ti // (2 * m) == tj // (2 * m)) & (ti // m != tj // m))
        m *= 2

    pairs = range(n_pair)
    lanes = [slice(p * PAIR, (p + 1) * PAIR) for p in pairs]

    def chunk_operands(g, carry):
        chunk_ids = [g * WKV_GROUP + q for q in range(WKV_GROUP)]
        chunk_rows = [pl.ds(pl.multiple_of(c * CHUNK, CHUNK), CHUNK) for c in chunk_ids]
        chunk_cum = [_split_dot(lw_ref[rows, :], cum_tri, 3, lhs=True) for rows in chunk_rows]
        chains = [(chunk_ids[q], p) for q in range(WKV_GROUP) for p in pairs]
        ops = []
        for i, (c, p) in enumerate(chains):
            q = i // n_pair
            rows = chunk_rows[q]
            r = r_ref[rows, lanes[p]].astype(F32)
            v = v_ref[rows, lanes[p]].astype(F32)
            kk = kk_ref[rows, lanes[p]].astype(F32)
            kd = kd_ref[rows, lanes[p]].astype(F32)
            b = b_ref[rows, lanes[p]].astype(F32)
            lw = lw_ref[rows, lanes[p]]

            cum = chunk_cum[q][:, lanes[p]]
            tot = cum[end_row:end_row + 1, :]
            e_in = jnp.exp(cum)
            e_ex = jnp.exp(cum - lw)
            e_inv = jnp.exp(-cum)
            e_end = jnp.exp(tot - cum)
            rt = _stack(r * e_in, lane_lo).astype(BF16)
            vs = _stack(v, lane_lo).astype(BF16)
            vs_s[c, p] = vs
            bk_s[c, p] = jnp.concatenate([_stack(b * e_end, lane_lo), _stack(kd * e_end, lane_lo)],
                                         axis=0).astype(BF16)
            dec_s[c, p] = jnp.exp(tot)
            ops.append(dict(
                rt=rt, vs=vs,
                at=_stack(-kk * e_ex, lane_lo).astype(BF16),
                bt=_stack(b * e_inv, lane_lo).astype(BF16),
                kt=_stack(kd * e_inv, lane_lo).astype(BF16)))

        m1 = [_dot_nt(jnp.concatenate([o['at'], o['rt']], axis=0),
                      jnp.concatenate([o['bt'], o['kt']], axis=0)) for o in ops]
        l_ab = [m[:STACK, :STACK] for m in m1]
        l_ak = [jnp.where(strict, m[:STACK, STACK:], 0.0).astype(BF16) for m in m1]
        for (c, p), m in zip(chains, m1):
            ar_s[c, p] = jnp.where(incl2, m[STACK:, :], 0.0).astype(BF16)

        inv = [eye + jnp.where(level_masks[0], l, 0.0) for l in l_ab]
        for mask in level_masks[1:]:
            inv_b = [x.astype(BF16) for x in inv]
            t = [_dot(jnp.where(mask, l, 0.0).astype(BF16), x).astype(BF16) for l, x in zip(l_ab, inv_b)]
            inv = [x + _dot(xb, y) for x, xb, y in zip(inv, inv_b, t)]
        inv_b = [x.astype(BF16) for x in inv]

        lakv = [_dot(l, o['vs']).astype(BF16) for l, o in zip(l_ak, ops)]
        for (c, p), x, o, y in zip(chains, inv_b, ops, lakv):
            wu = _dot(x, jnp.concatenate([o['at'], y], axis=1))
            wr_s[c, p] = jnp.concatenate([wu[:, :PAIR].astype(BF16), o['rt']], axis=0)
            u0_s[c, p] = wu[:, PAIR:]
        return carry

    lax.fori_loop(0, n_sub // WKV_GROUP, chunk_operands, 0)

    def advance(step, carry):
        c = (n_sub - 1 - step) if reverse else step
        rows = pl.ds(pl.multiple_of(c * CHUNK, CHUNK), CHUNK)
        s = [s_ref[p] for p in pairs]
        ur = [_dot_nt(wr_s[c, p], s[p].astype(BF16)) for p in pairs]
        uv = [jnp.concatenate([(ur[p][:STACK] + u0_s[c, p]).astype(BF16), vs_s[c, p]], axis=0)
              for p in pairs]
        for p in pairs:
            s_ref[p] = s[p] * dec_s[c, p] + _dot_tn(uv[p], bk_s[c, p])
        for p in pairs:
            ys = ur[p][STACK:] + _dot(ar_s[c, p], uv[p])
            y_ref[rows, lanes[p]] = ys[:CHUNK] + ys[CHUNK:]
        return carry

    lax.fori_loop(0, n_sub, advance, 0)


def _wkv(r, v, kk, kd, bb, lw, reverse):
    b, t, w = r.shape
    lc = _pick_tile(t, WKV_BLOCK)
    n_t = t // lc
    n_sub = lc // CHUNK
    n_pair = w // PAIR
    tmap = (lambda bi, ti: (bi, n_t - 1 - ti, 0)) if reverse else (lambda bi, ti: (bi, ti, 0))
    seq_spec = pl.BlockSpec((None, lc, w), tmap)
    per_chain = lambda rows, cols, dtype: pltpu.VMEM((n_sub, n_pair, rows, cols), dtype)
    return pl.pallas_call(
        functools.partial(_wkv_kernel, reverse),
        grid=(b, n_t),
        in_specs=[seq_spec] * 6,
        out_specs=seq_spec,
        out_shape=jax.ShapeDtypeStruct((b, t, w), F32),
        scratch_shapes=[
            pltpu.VMEM((n_pair, PAIR, PAIR), F32),
            per_chain(2 * STACK, PAIR, BF16),
            per_chain(STACK, PAIR, F32),
            per_chain(STACK, 2 * STACK, BF16),
            per_chain(STACK, PAIR, BF16),
            per_chain(2 * STACK, PAIR, BF16),
            per_chain(1, PAIR, F32),
        ],
        compiler_params=_params("parallel", "arbitrary"),
    )(r, v, kk, kd, bb, lw)


def _rwkv_post_kernel(yf_ref, yb_ref, r_ref, v_ref, kdf_ref, kdb_ref, g_ref,
                      bonus_ref, gnw_ref, gnb_ref, o_ref):
    n_pair = o_ref.shape[1] // PAIR
    head_ones = _head_ones()
    pairs = range(n_pair)
    lanes = [slice(p * PAIR, (p + 1) * PAIR) for p in pairs]
    y = [yf_ref[:, l] + yb_ref[:, l] for l in lanes]
    mean = [_split_dot(x, head_ones, 2) * (1.0 / HEAD_DIM) for x in y]
    yc = [x - m for x, m in zip(y, mean)]
    var = [_split_dot(x * x, head_ones, 2) * (1.0 / HEAD_DIM) for x in yc]
    rk = [r_ref[:, l].astype(F32) * (0.5 * (kdf_ref[:, l].astype(F32) + kdb_ref[:, l].astype(F32)))
          * bonus_ref[:, l] for l in lanes]
    bonus = [_split_dot(x, head_ones, 2) for x in rk]
    for p in pairs:
        l = lanes[p]
        yn = yc[p] * lax.rsqrt(var[p] + GN_EPS) * gnw_ref[:, l] + gnb_ref[:, l]
        o_ref[:, l] = ((yn + bonus[p] * v_ref[:, l].astype(F32)) * g_ref[:, l].astype(F32)).astype(o_ref.dtype)


def _rwkv_post(yf, yb, r, v, kdf, kdb, g, bonus_scale, gn_w, gn_b):
    n, w = yf.shape
    tm = _pick_tile(n, 256)
    tile = pl.BlockSpec((tm, w), lambda i: (i, 0))
    row = pl.BlockSpec((1, w), lambda i: (0, 0))
    return pl.pallas_call(
        _rwkv_post_kernel,
        grid=(n // tm,),
        in_specs=[tile] * 7 + [row] * 3,
        out_specs=tile,
        out_shape=jax.ShapeDtypeStruct((n, w), BF16),
        compiler_params=_params("parallel"),
    )(yf, yb, r, v, kdf, kdb, g, bonus_scale, gn_w, gn_b)


def _nat_bias_table(rpb):
    off = jnp.arange(WIN_H)
    row_rel = off[None, :] - off[:, None] + (WIN_H - 1)
    cols = jnp.arange(GRID_W)
    col_start = jnp.clip(cols - WIN_W // 2, 0, GRID_W - WIN_W)
    in_win = (cols[None, :] >= col_start[:, None]) & (cols[None, :] < col_start[:, None] + WIN_W)
    col_rel = cols[None, :] - cols[:, None] + (WIN_W - 1)
    pick = (in_win[:, :, None] & (col_rel[:, :, None] == jnp.arange(2 * WIN_W - 1))).astype(F32)
    tbl = jnp.einsum('hdar,jcr->hdjac', rpb.astype(F32)[:, row_rel], pick, precision=HIGHEST)
    tbl = jnp.where(in_win[None, None, :, None, :], tbl, MASK_VALUE)
    return tbl.reshape(rpb.shape[0], WIN_H, GRID_W, WIN_H * GRID_W)


def _nat_row_start(i, n_rows):
    return jnp.clip(i - WIN_H // 2, 0, n_rows - WIN_H)


def _natten_kernel(n_rows, q_ref, k_ref, v_ref, bias_ref, o_ref):
    i = pl.program_id(1)
    n_pair = q_ref.shape[1] // PAIR
    n_keys = WIN_H * GRID_W
    key_rows = pl.ds(pl.multiple_of(_nat_row_start(i, n_rows) * GRID_W, GRID_W), n_keys)
    lane_lo = lax.broadcasted_iota(jnp.int32, (GRID_W, PAIR), 1) < HEAD_DIM
    scale = HEAD_DIM ** -0.5
    heads = range(2 * n_pair)
    lanes = [slice((h // 2) * PAIR, (h // 2 + 1) * PAIR) for h in heads]
    s = []
    for h in heads:
        q = q_ref[:, lanes[h]] * scale
        q_h = jnp.where(lane_lo if h % 2 == 0 else ~lane_lo, q, jnp.zeros_like(q))
        s.append(_dot_nt(q_h, k_ref[key_rows, lanes[h]]) + bias_ref[h, 0])
    e = [jnp.exp(x - jnp.max(x, axis=-1, keepdims=True)) for x in s]
    denom = [jnp.sum(x, axis=-1, keepdims=True) for x in e]
    outs = [_dot(e[h].astype(BF16), v_ref[key_rows, lanes[h]]) / denom[h] for h in heads]
    for p in range(n_pair):
        o_ref[:, lanes[2 * p]] = jnp.where(lane_lo, outs[2 * p], outs[2 * p + 1]).astype(o_ref.dtype)


def _natten(qkv, bias_tbl):
    b, t, w3 = qkv.shape
    w = w3 // 3
    n_rows = t // GRID_W
    n_heads = w // HEAD_DIM
    bias_map = lambda bi, i: (0, i - _nat_row_start(i, n_rows), 0, 0)
    return pl.pallas_call(
        functools.partial(_natten_kernel, n_rows),
        grid=(b, n_rows),
        in_specs=[
            pl.BlockSpec((None, GRID_W, w), lambda bi, i: (bi, i, 0)),
            pl.BlockSpec((None, t, w), lambda bi, i: (bi, 0, 1)),
            pl.BlockSpec((None, t, w), lambda bi, i: (bi, 0, 2)),
            pl.BlockSpec((n_heads, 1, GRID_W, WIN_H * GRID_W), bias_map),
        ],
        out_specs=pl.BlockSpec((None, GRID_W, w), lambda bi, i: (bi, i, 0)),
        out_shape=jax.ShapeDtypeStruct((b, t, w), BF16),
        compiler_params=_params("parallel", "arbitrary"),
    )(qkv, qkv, qkv, bias_tbl)


def _out_proj_kernel(x_ref, oa_ref, ob_ref, wa_ref, wb_ref, g_ref, o_ref):
    o = _dot(oa_ref[...], wa_ref[...]) + _dot(ob_ref[...], wb_ref[...])
    o_ref[...] = x_ref[...] + _rms(o, g_ref[...])


def _out_proj(x, o_a, o_b, w_a, w_b, g):
    n, d = x.shape
    w = o_a.shape[1]
    tm = _pick_tile(n, 256)
    return pl.pallas_call(
        _out_proj_kernel,
        grid=(n // tm,),
        in_specs=[
            pl.BlockSpec((tm, d), lambda i: (i, 0)),
            pl.BlockSpec((tm, w), lambda i: (i, 0)),
            pl.BlockSpec((tm, w), lambda i: (i, 0)),
            pl.BlockSpec((w, d), lambda i: (0, 0)),
            pl.BlockSpec((w, d), lambda i: (0, 0)),
            pl.BlockSpec((1, d), lambda i: (0, 0)),
        ],
        out_specs=pl.BlockSpec((tm, d), lambda i: (i, 0)),
        out_shape=jax.ShapeDtypeStruct((n, d), F32),
        compiler_params=_params("parallel"),
    )(x, o_a, o_b, w_a, w_b, g)


def _encoder_layer(x, p, bias_tbl):
    b, t, d = x.shape
    n = b * t
    w = p['decay_up_fwd'].shape[1]
    x = x.reshape(n, d)
    x = _ffn(x, p['ffn1_pre_g'], p['ffn1_post_g'], p['ffn1_w_gate'], p['ffn1_w_up'], p['ffn1_w_down'])

    z_rwkv = _norm_proj(x, p['mix_pre_g'], p['w_in_rwkv'], BF16)
    qkv = _norm_proj(x, p['mix_pre_g'], p['w_in_nat'], BF16)

    r, v, kk, kdf, kdb, bf, bb, lwf, lwb, g = _rwkv_prep(
        z_rwkv, t, p['rwkv_shift_mix'],
        p['decay_bias_fwd'], p['decay_up_fwd'], p['decay_bias_bwd'], p['decay_up_bwd'],
        p['iclr_bias_fwd'], p['iclr_up_fwd'], p['iclr_bias_bwd'], p['iclr_up_bwd'], p['gate_up'],
        p['key_norm_scale'], p['key_iclr_mix'])
    seq = lambda a: a.reshape(b, t, w)
    yf = _wkv(seq(r), seq(v), seq(kk), seq(kdf), seq(bf), seq(lwf), False)
    yb = _wkv(seq(r), seq(v), seq(kk), seq(kdb), seq(bb), seq(lwb), True)
    o_rwkv = _rwkv_post(yf.reshape(n, w), yb.reshape(n, w), r, v, kdf, kdb, g,
                        p['bonus_scale'], p['gn_w'], p['gn_b'])

    o_nat = _natten(qkv.reshape(b, t, -1), bias_tbl).reshape(n, -1)

    x = _out_proj(x, o_rwkv, o_nat, p['w_out_rwkv'], p['w_out_nat'], p['mix_post_g'])
    x = _ffn(x, p['ffn2_pre_g'], p['ffn2_post_g'], p['ffn2_w_gate'], p['ffn2_w_up'], p['ffn2_w_down'])
    return x.reshape(b, t, d)


_ROW_PARAMS = ('ffn1_pre_g', 'ffn1_post_g', 'mix_pre_g', 'rwkv_shift_mix',
               'decay_bias_fwd', 'decay_bias_bwd', 'iclr_bias_fwd', 'iclr_bias_bwd',
               'key_norm_scale', 'key_iclr_mix', 'bonus_scale', 'gn_w', 'gn_b',
               'mix_post_g', 'ffn2_pre_g', 'ffn2_post_g')
_BF16_MATS = ('ffn1_w_gate', 'ffn1_w_up', 'ffn1_w_down', 'ffn2_w_gate', 'ffn2_w_up', 'ffn2_w_down',
              'decay_up_fwd', 'decay_up_bwd', 'iclr_up_fwd', 'iclr_up_bwd', 'gate_up')


def _layer_params(weights, l):
    p = {name: weights[name][l].reshape(1, -1) for name in _ROW_PARAMS}
    p.update({name: weights[name][l].astype(BF16) for name in _BF16_MATS})
    w = weights['decay_up_fwd'].shape[-1]
    rwkv_cols = 3 * w + DECAY_RANK + ICLR_RANK + GATE_RANK
    w_in = weights['w_in'][l].astype(BF16)
    p['w_in_rwkv'] = w_in[:, :rwkv_cols]
    p['w_in_nat'] = w_in[:, rwkv_cols:]
    w_out = weights['w_out'][l].astype(BF16)
    p['w_out_rwkv'] = w_out[:w]
    p['w_out_nat'] = w_out[w:]
    return p, _nat_bias_table(weights['nat_rpb'][l])


def kernel(x_prompt, x_sample, ffn1_pre_g, ffn1_post_g, ffn1_w_gate, ffn1_w_up, ffn1_w_down, mix_pre_g, w_in, rwkv_shift_mix, decay_bias_fwd, decay_up_fwd, decay_bias_bwd, decay_up_bwd, iclr_bias_fwd, iclr_up_fwd, iclr_bias_bwd, iclr_up_bwd, gate_up, key_norm_scale, key_iclr_mix, bonus_scale, gn_w, gn_b, nat_rpb, w_out, mix_post_g, ffn2_pre_g, ffn2_post_g, ffn2_w_gate, ffn2_w_up, ffn2_w_down):
    weights = dict(
        ffn1_pre_g=ffn1_pre_g, ffn1_post_g=ffn1_post_g, ffn1_w_gate=ffn1_w_gate, ffn1_w_up=ffn1_w_up,
        ffn1_w_down=ffn1_w_down, mix_pre_g=mix_pre_g, w_in=w_in, rwkv_shift_mix=rwkv_shift_mix,
        decay_bias_fwd=decay_bias_fwd, decay_up_fwd=decay_up_fwd, decay_bias_bwd=decay_bias_bwd,
        decay_up_bwd=decay_up_bwd, iclr_bias_fwd=iclr_bias_fwd, iclr_up_fwd=iclr_up_fwd,
        iclr_bias_bwd=iclr_bias_bwd, iclr_up_bwd=iclr_up_bwd, gate_up=gate_up,
        key_norm_scale=key_norm_scale, key_iclr_mix=key_iclr_mix, bonus_scale=bonus_scale,
        gn_w=gn_w, gn_b=gn_b, nat_rpb=nat_rpb, w_out=w_out, mix_post_g=mix_post_g,
        ffn2_pre_g=ffn2_pre_g, ffn2_post_g=ffn2_post_g, ffn2_w_gate=ffn2_w_gate, ffn2_w_up=ffn2_w_up,
        ffn2_w_down=ffn2_w_down)
    depth = w_in.shape[0]
    layers = [_layer_params(weights, l) for l in range(depth)]
    outs = []
    for x in (x_prompt, x_sample):
        for p, bias_tbl in layers:
            x = _encoder_layer(x, p, bias_tbl)
        outs.append(x)
    return tuple(outs)
```

```python
import functools
import math

import jax
import jax.numpy as jnp
from jax import lax
from jax.experimental import pallas as pl
from jax.experimental.pallas import tpu as pltpu

F32 = jnp.float32
BF16 = jnp.bfloat16
HIGHEST = lax.Precision.HIGHEST

HEAD_DIM = 64
GRID_W = 64
WIN_H = 8
WIN_W = 16
DECAY_RANK = 64
ICLR_RANK = 64
GATE_RANK = 128
RMS_EPS = 1e-6
GN_EPS = 64e-5
DECAY_OFFSET = 0.5
FFN_RESIDUAL = 0.5
MASK_VALUE = -1e30

LANES = 128
HALO_ROWS = 16
EDGE_ROWS = 8
VMEM_LIMIT_BYTES = 56 * 1024 * 1024

PAIR = 2 * HEAD_DIM
CHUNK = 64
STACK = 2 * CHUNK
WKV_BLOCK = 512
WKV_GROUP = 2


def _params(*semantics):
    return pltpu.CompilerParams(dimension_semantics=semantics, vmem_limit_bytes=VMEM_LIMIT_BYTES)


def _dot(a, b, precision=None):
    return jnp.dot(a, b, preferred_element_type=F32, precision=precision)


def _dot_nt(a, b):
    return lax.dot_general(a, b, (((1,), (1,)), ((), ())), preferred_element_type=F32)


def _dot_tn(a, b):
    return lax.dot_general(a, b, (((0,), (0,)), ((), ())), preferred_element_type=F32)


def _rms(x, g):
    return x * lax.rsqrt(jnp.mean(x * x, axis=-1, keepdims=True) + RMS_EPS) * g


def _head_ones():
    return jnp.where(
        (lax.broadcasted_iota(jnp.int32, (PAIR, PAIR), 0) // HEAD_DIM)
        == (lax.broadcasted_iota(jnp.int32, (PAIR, PAIR), 1) // HEAD_DIM), 1.0, 0.0).astype(BF16)


def _split_dot(x, w, terms, lhs=False):
    acc = None
    rem = x
    for t in range(terms):
        piece = rem.astype(BF16)
        d = _dot(w, piece) if lhs else _dot(piece, w)
        acc = d if acc is None else acc + d
        if t + 1 < terms:
            rem = rem - piece.astype(F32)
    return acc


def _pick_tile(n, target):
    t = min(n, target)
    while n % t:
        t //= 2
    return t


def _ffn_kernel(x_ref, gpre_ref, gpost_ref, wg_ref, wu_ref, wd_ref, o_ref, h_ref, acc_ref):
    j = pl.program_id(1)

    @pl.when(j == 0)
    def _():
        h_ref[...] = _rms(x_ref[...], gpre_ref[...]).astype(BF16)
        acc_ref[...] = jnp.zeros_like(acc_ref)

    h = h_ref[...]
    g = _dot(h, wg_ref[...])
    u = _dot(h, wu_ref[...])
    act = (g * jax.nn.sigmoid(g) * u).astype(BF16)
    acc_ref[...] += _dot(act, wd_ref[...])

    @pl.when(j == pl.num_programs(1) - 1)
    def _():
        o_ref[...] = x_ref[...] + FFN_RESIDUAL * _rms(acc_ref[...], gpost_ref[...])


def _ffn(x, g_pre, g_post, w_gate, w_up, w_down):
    n, d = x.shape
    f = w_gate.shape[1]
    tm = _pick_tile(n, 512)
    tf = _pick_tile(f, 512)
    return pl.pallas_call(
        _ffn_kernel,
        grid=(n // tm, f // tf),
        in_specs=[
            pl.BlockSpec((tm, d), lambda i, j: (i, 0)),
            pl.BlockSpec((1, d), lambda i, j: (0, 0)),
            pl.BlockSpec((1, d), lambda i, j: (0, 0)),
            pl.BlockSpec((d, tf), lambda i, j: (0, j)),
            pl.BlockSpec((d, tf), lambda i, j: (0, j)),
            pl.BlockSpec((tf, d), lambda i, j: (j, 0)),
        ],
        out_specs=pl.BlockSpec((tm, d), lambda i, j: (i, 0)),
        out_shape=jax.ShapeDtypeStruct((n, d), F32),
        scratch_shapes=[pltpu.VMEM((tm, d), BF16), pltpu.VMEM((tm, d), F32)],
        compiler_params=_params("parallel", "arbitrary"),
    )(x, g_pre, g_post, w_gate, w_up, w_down)


def _norm_proj_kernel(x_ref, g_ref, wa_ref, wb_ref, oa_ref, ob_ref):
    h = _rms(x_ref[...], g_ref[...]).astype(BF16)
    oa_ref[...] = _dot(h, wa_ref[...]).astype(oa_ref.dtype)
    ob_ref[...] = _dot(h, wb_ref[...]).astype(ob_ref.dtype)


def _norm_proj(x, g, w_a, w_b):
    n, d = x.shape
    ca, cb = w_a.shape[1], w_b.shape[1]
    tm = _pick_tile(n, 256)
    resident = lambda cols: pl.BlockSpec((d, cols), lambda i: (0, 0), pipeline_mode=pl.Buffered(1))
    return pl.pallas_call(
        _norm_proj_kernel,
        grid=(n // tm,),
        in_specs=[
            pl.BlockSpec((tm, d), lambda i: (i, 0)),
            pl.BlockSpec((1, d), lambda i: (0, 0)),
            resident(ca),
            resident(cb),
        ],
        out_specs=[pl.BlockSpec((tm, ca), lambda i: (i, 0)), pl.BlockSpec((tm, cb), lambda i: (i, 0))],
        out_shape=[jax.ShapeDtypeStruct((n, ca), BF16), jax.ShapeDtypeStruct((n, cb), BF16)],
        compiler_params=_params("parallel"),
    )(x, g, w_a, w_b)


def _rwkv_prep_kernel(seq_len, z_ref, zp_ref, zn_ref, mu_ref,
                      dbf_ref, duf_ref, dbb_ref, dub_ref, ibf_ref, iuf_ref, ibb_ref, iub_ref, gu_ref,
                      kns_ref, mix_ref,
                      r_ref, v_ref, kk_ref, kdf_ref, kdb_ref, bf_ref, bb_ref, lwf_ref, lwb_ref, g_ref, u_ref):
    i = pl.program_id(0)
    tm = z_ref.shape[0]
    w = r_ref.shape[1]
    ti = lax.broadcasted_iota(jnp.int32, (tm, tm), 0)
    tj = lax.broadcasted_iota(jnp.int32, (tm, tm), 1)
    neighbours = jnp.where((ti == tj + 1) | (ti + 1 == tj), 1.0, 0.0).astype(BF16)
    z = z_ref[...]
    mu = mu_ref[...]
    u_ref[...] = z + mu * (0.5 * _dot(neighbours, z) - z)
    at_seq_start = (i * tm) % seq_len == 0
    at_seq_end = ((i + 1) * tm) % seq_len == 0
    prev_row = jnp.where(at_seq_start, 0.0, zp_ref[HALO_ROWS - 1:HALO_ROWS, :].astype(F32))
    next_row = jnp.where(at_seq_end, 0.0, zn_ref[0:1, :].astype(F32))
    edge = lax.broadcasted_iota(jnp.int32, (EDGE_ROWS, 1), 0)
    u_ref[0:EDGE_ROWS, :] += jnp.where(edge == 0, 0.5 * mu * prev_row, 0.0)
    u_ref[tm - EDGE_ROWS:tm, :] += jnp.where(edge == EDGE_ROWS - 1, 0.5 * mu * next_row, 0.0)

    r_ref[...] = u_ref[:, 0:w].astype(r_ref.dtype)
    v_ref[...] = u_ref[:, 2 * w:3 * w].astype(v_ref.dtype)
    o = 3 * w
    dec_lo = jnp.tanh(u_ref[:, o:o + DECAY_RANK]).astype(BF16)
    o += DECAY_RANK
    iclr_lo = u_ref[:, o:o + ICLR_RANK].astype(BF16)
    o += ICLR_RANK
    gate_lo = jax.nn.sigmoid(u_ref[:, o:o + GATE_RANK]).astype(BF16)

    scale = -math.exp(-DECAY_OFFSET)
    lwf_ref[...] = scale * jax.nn.sigmoid(dbf_ref[...] + _dot(dec_lo, duf_ref[...]))
    lwb_ref[...] = scale * jax.nn.sigmoid(dbb_ref[...] + _dot(dec_lo, dub_ref[...]))
    g_ref[...] = _dot(gate_lo, gu_ref[...]).astype(g_ref.dtype)

    head_ones = _head_ones()
    for p in range(w // PAIR):
        lanes = slice(p * PAIR, (p + 1) * PAIR)
        k = u_ref[:, w + p * PAIR:w + (p + 1) * PAIR]
        kn = k * kns_ref[:, lanes]
        norm = jnp.sqrt(_split_dot(kn * kn, head_ones, 2))
        kk = kn / jnp.maximum(norm, 1e-12)
        kk_ref[:, lanes] = kk.astype(kk_ref.dtype)
        mix = mix_ref[:, lanes]
        for ib_ref, iu_ref, kd_ref, b_ref in ((ibf_ref, iuf_ref, kdf_ref, bf_ref),
                                              (ibb_ref, iub_ref, kdb_ref, bb_ref)):
            al = jax.nn.sigmoid(ib_ref[:, lanes] + _dot(iclr_lo, iu_ref[:, lanes]))
            kd_ref[:, lanes] = (k * (1.0 + (al - 1.0) * mix)).astype(kd_ref.dtype)
            b_ref[:, lanes] = (kk * al).astype(b_ref.dtype)


def _rwkv_prep(z, seq_len, mu, dbf, duf, dbb, dub, ibf, iuf, ibb, iub, gate_up, kns, mix):
    n, c = z.shape
    assert z.dtype == BF16
    w = duf.shape[1]
    tm = _pick_tile(seq_len, 256)
    hb = tm // HALO_ROWS
    last_hb = n // HALO_ROWS - 1
    row_spec = lambda cols: pl.BlockSpec((1, cols), lambda i: (0, 0))
    mat_spec = lambda rows: pl.BlockSpec((rows, w), lambda i: (0, 0))
    out_spec = pl.BlockSpec((tm, w), lambda i: (i, 0))
    outs = [jax.ShapeDtypeStruct((n, w), dt) for dt in (BF16,) * 7 + (F32, F32, BF16)]
    return pl.pallas_call(
        functools.partial(_rwkv_prep_kernel, seq_len),
        grid=(n // tm,),
        in_specs=[
            pl.BlockSpec((tm, c), lambda i: (i, 0)),
            pl.BlockSpec((HALO_ROWS, c), lambda i: (jnp.maximum(i * hb - 1, 0), 0)),
            pl.BlockSpec((HALO_ROWS, c), lambda i: (jnp.minimum((i + 1) * hb, last_hb), 0)),
            row_spec(c),
            row_spec(w), mat_spec(DECAY_RANK), row_spec(w), mat_spec(DECAY_RANK),
            row_spec(w), mat_spec(ICLR_RANK), row_spec(w), mat_spec(ICLR_RANK),
            mat_spec(GATE_RANK), row_spec(w), row_spec(w),
        ],
        out_specs=[out_spec] * 10,
        out_shape=outs,
        scratch_shapes=[pltpu.VMEM((tm, c), F32)],
        compiler_params=_params("parallel"),
    )(z, z, z, mu, dbf, duf, dbb, dub, ibf, iuf, ibb, iub, gate_up, kns, mix)


def _stack(x, lane_lo):
    return jnp.concatenate([jnp.where(lane_lo, x, 0.0), jnp.where(lane_lo, 0.0, x)], axis=0)


def _wkv_kernel(reverse, r_ref, v_ref, kk_ref, kd_ref, b_ref, lw_ref, y_ref,
                s_ref, wr_s, u0_s, ar_s, vs_s, bk_s, dec_s):
    tc = pl.program_id(1)
    n_sub = r_ref.shape[0] // CHUNK
    n_pair = r_ref.shape[1] // PAIR

    @pl.when(tc == 0)
    def _():
        s_ref[...] = jnp.zeros_like(s_ref)

    ii = lax.broadcasted_iota(jnp.int32, (STACK, STACK), 0)
    jj = lax.broadcasted_iota(jnp.int32, (STACK, STACK), 1)
    ti = ii % CHUNK
    tj = jj % CHUNK
    same_head = (ii // CHUNK) == (jj // CHUNK)
    ii2 = lax.broadcasted_iota(jnp.int32, (STACK, 2 * STACK), 0)
    jj2 = lax.broadcasted_iota(jnp.int32, (STACK, 2 * STACK), 1)
    same_head2 = (ii2 // CHUNK) == ((jj2 % STACK) // CHUNK)
    if reverse:
        strict = same_head & (ti < tj)
        incl2 = same_head2 & (ii2 % CHUNK <= jj2 % CHUNK)
    else:
        strict = same_head & (ti > tj)
        incl2 = same_head2 & (ii2 % CHUNK >= jj2 % CHUNK)
    ci = lax.broadcasted_iota(jnp.int32, (CHUNK, CHUNK), 0)
    cj = lax.broadcasted_iota(jnp.int32, (CHUNK, CHUNK), 1)
    cum_tri = jnp.where((ci <= cj) if reverse else (ci >= cj), 1.0, 0.0).astype(BF16)
    eye = jnp.where(ii == jj, 1.0, 0.0).astype(F32)
    lane_lo = lax.broadcasted_iota(jnp.int32, (CHUNK, PAIR), 1) < HEAD_DIM
    end_row = 0 if reverse else CHUNK - 1

    level_masks = []
    m = 1
    while m < CHUNK:
        level_masks.append(strict & (ti // (2 * m) == tj // (2 * m)) & (ti // m != tj // m))
        m *= 2

    pairs = range(n_pair)
    lanes = [slice(p * PAIR, (p + 1) * PAIR) for p in pairs]

    def chunk_operands(g, carry):
        chunk_ids = [g * WKV_GROUP + q for q in range(WKV_GROUP)]
        chunk_rows = [pl.ds(pl.multiple_of(c * CHUNK, CHUNK), CHUNK) for c in chunk_ids]
        chunk_cum = [_split_dot(lw_ref[rows, :], cum_tri, 3, lhs=True) for rows in chunk_rows]
        chains = [(chunk_ids[q], p) for q in range(WKV_GROUP) for p in pairs]
        ops = []
        for i, (c, p) in enumerate(chains):
            q = i // n_pair
            rows = chunk_rows[q]
            r = r_ref[rows, lanes[p]].astype(F32)
            v = v_ref[rows, lanes[p]].astype(F32)
            kk = kk_ref[rows, lanes[p]].astype(F32)
            kd = kd_ref[rows, lanes[p]].astype(F32)
            b = b_ref[rows, lanes[p]].astype(F32)
            lw = lw_ref[rows, lanes[p]]

            cum = chunk_cum[q][:, lanes[p]]
            tot = cum[end_row:end_row + 1, :]
            e_in = jnp.exp(cum)
            e_ex = jnp.exp(cum - lw)
            e_inv = jnp.exp(-cum)
            e_end = jnp.exp(tot - cum)
            rt = _stack(r * e_in, lane_lo).astype(BF16)
            vs = _stack(v, lane_lo).astype(BF16)
            vs_s[c, p] = vs
            bk_s[c, p] = jnp.concatenate([_stack(b * e_end, lane_lo), _stack(kd * e_end, lane_lo)],
                                         axis=0).astype(BF16)
            dec_s[c, p] = jnp.exp(tot)
            ops.append(dict(
                rt=rt, vs=vs,
                at=_stack(-kk * e_ex, lane_lo).astype(BF16),
                bt=_stack(b * e_inv, lane_lo).astype(BF16),
                kt=_stack(kd * e_inv, lane_lo).astype(BF16)))

        m1 = [_dot_nt(jnp.concatenate([o['at'], o['rt']], axis=0),
                      jnp.concatenate([o['bt'], o['kt']], axis=0)) for o in ops]
        l_ab = [m[:STACK, :STACK] for m in m1]
        l_ak = [jnp.where(strict, m[:STACK, STACK:], 0.0).astype(BF16) for m in m1]
        for (c, p), m in zip(chains, m1):
            ar_s[c, p] = jnp.where(incl2, m[STACK:, :], 0.0).astype(BF16)

        inv = [eye + jnp.where(level_masks[0], l, 0.0) for l in l_ab]
        for mask in level_masks[1:]:
            inv_b = [x.astype(BF16) for x in inv]
            t = [_dot(jnp.where(mask, l, 0.0).astype(BF16), x).astype(BF16) for l, x in zip(l_ab, inv_b)]
            inv = [x + _dot(xb, y) for x, xb, y in zip(inv, inv_b, t)]
        inv_b = [x.astype(BF16) for x in inv]

        lakv = [_dot(l, o['vs']).astype(BF16) for l, o in zip(l_ak, ops)]
        for (c, p), x, o, y in zip(chains, inv_b, ops, lakv):
            wu = _dot(x, jnp.concatenate([o['at'], y], axis=1))
            wr_s[c, p] = jnp.concatenate([wu[:, :PAIR].astype(BF16), o['rt']], axis=0)
            u0_s[c, p] = wu[:, PAIR:]
        return carry

    lax.fori_loop(0, n_sub // WKV_GROUP, chunk_operands, 0)

    def advance(step, carry):
        c = (n_sub - 1 - step) if reverse else step
        rows = pl.ds(pl.multiple_of(c * CHUNK, CHUNK), CHUNK)
        s = [s_ref[p] for p in pairs]
        ur = [_dot_nt(wr_s[c, p], s[p].astype(BF16)) for p in pairs]
        uv = [jnp.concatenate([(ur[p][:STACK] + u0_s[c, p]).astype(BF16), vs_s[c, p]], axis=0)
              for p in pairs]
        for p in pairs:
            s_ref[p] = s[p] * dec_s[c, p] + _dot_tn(uv[p], bk_s[c, p])
        for p in pairs:
            ys = ur[p][STACK:] + _dot(ar_s[c, p], uv[p])
            y_ref[rows, lanes[p]] = ys[:CHUNK] + ys[CHUNK:]
        return carry

    lax.fori_loop(0, n_sub, advance, 0, unroll=True)


def _wkv(r, v, kk, kd, bb, lw, reverse):
    b, t, w = r.shape
    lc = _pick_tile(t, WKV_BLOCK)
    n_t = t // lc
    n_sub = lc // CHUNK
    n_pair = w // PAIR
    tmap = (lambda bi, ti: (bi, n_t - 1 - ti, 0)) if reverse else (lambda bi, ti: (bi, ti, 0))
    seq_spec = pl.BlockSpec((None, lc, w), tmap)
    per_chain = lambda rows, cols, dtype: pltpu.VMEM((n_sub, n_pair, rows, cols), dtype)
    return pl.pallas_call(
        functools.partial(_wkv_kernel, reverse),
        grid=(b, n_t),
        in_specs=[seq_spec] * 6,
        out_specs=seq_spec,
        out_shape=jax.ShapeDtypeStruct((b, t, w), F32),
        scratch_shapes=[
            pltpu.VMEM((n_pair, PAIR, PAIR), F32),
            per_chain(2 * STACK, PAIR, BF16),
            per_chain(STACK, PAIR, F32),
            per_chain(STACK, 2 * STACK, BF16),
            per_chain(STACK, PAIR, BF16),
            per_chain(2 * STACK, PAIR, BF16),
            per_chain(1, PAIR, F32),
        ],
        compiler_params=_params("parallel", "arbitrary"),
    )(r, v, kk, kd, bb, lw)


def _nat_bias_table(rpb):
    off = jnp.arange(WIN_H)
    row_rel = off[None, :] - off[:, None] + (WIN_H - 1)
    cols = jnp.arange(GRID_W)
    col_start = jnp.clip(cols - WIN_W // 2, 0, GRID_W - WIN_W)
    in_win = (cols[None, :] >= col_start[:, None]) & (cols[None, :] < col_start[:, None] + WIN_W)
    col_rel = cols[None, :] - cols[:, None] + (WIN_W - 1)
    pick = (in_win[:, :, None] & (col_rel[:, :, None] == jnp.arange(2 * WIN_W - 1))).astype(F32)
    tbl = jnp.einsum('hdar,jcr->hdjac', rpb.astype(F32)[:, row_rel], pick, precision=HIGHEST)
    tbl = jnp.where(in_win[None, None, :, None, :], tbl, MASK_VALUE)
    return tbl.reshape(rpb.shape[0], WIN_H, GRID_W, WIN_H * GRID_W)


def _nat_row_start(i, n_rows):
    return jnp.clip(i - WIN_H // 2, 0, n_rows - WIN_H)


def _natten_kernel(n_rows, q_ref, k_ref, v_ref, bias_ref, o_ref):
    i = pl.program_id(1)
    n_pair = q_ref.shape[1] // PAIR
    n_keys = WIN_H * GRID_W
    key_rows = pl.ds(pl.multiple_of(_nat_row_start(i, n_rows) * GRID_W, GRID_W), n_keys)
    lane_lo = lax.broadcasted_iota(jnp.int32, (GRID_W, PAIR), 1) < HEAD_DIM
    scale = HEAD_DIM ** -0.5
    heads = range(2 * n_pair)
    lanes = [slice((h // 2) * PAIR, (h // 2 + 1) * PAIR) for h in heads]
    s = []
    for h in heads:
        q = q_ref[:, lanes[h]] * scale
        q_h = jnp.where(lane_lo if h % 2 == 0 else ~lane_lo, q, jnp.zeros_like(q))
        s.append(_dot_nt(q_h, k_ref[key_rows, lanes[h]]) + bias_ref[h, 0])
    e = [jnp.exp(x - jnp.max(x, axis=-1, keepdims=True)) for x in s]
    denom = [jnp.sum(x, axis=-1, keepdims=True) for x in e]
    outs = [_dot(e[h].astype(BF16), v_ref[key_rows, lanes[h]]) / denom[h] for h in heads]
    for p in range(n_pair):
        o_ref[:, lanes[2 * p]] = jnp.where(lane_lo, outs[2 * p], outs[2 * p + 1]).astype(o_ref.dtype)


def _natten(qkv, bias_tbl):
    b, t, w3 = qkv.shape
    w = w3 // 3
    n_rows = t // GRID_W
    n_heads = w // HEAD_DIM
    bias_map = lambda bi, i: (0, i - _nat_row_start(i, n_rows), 0, 0)
    return pl.pallas_call(
        functools.partial(_natten_kernel, n_rows),
        grid=(b, n_rows),
        in_specs=[
            pl.BlockSpec((None, GRID_W, w), lambda bi, i: (bi, i, 0)),
            pl.BlockSpec((None, t, w), lambda bi, i: (bi, 0, 1)),
            pl.BlockSpec((None, t, w), lambda bi, i: (bi, 0, 2)),
            pl.BlockSpec((n_heads, 1, GRID_W, WIN_H * GRID_W), bias_map),
        ],
        out_specs=pl.BlockSpec((None, GRID_W, w), lambda bi, i: (bi, i, 0)),
        out_shape=jax.ShapeDtypeStruct((b, t, w), BF16),
        compiler_params=_params("parallel", "arbitrary"),
    )(qkv, qkv, qkv, bias_tbl)


def _mix_out_kernel(x_ref, yf_ref, yb_ref, r_ref, v_ref, kdf_ref, kdb_ref, g_ref, onat_ref,
                    bonus_ref, gnw_ref, gnb_ref, wa_ref, wb_ref, gpost_ref, o_ref):
    n_pair = yf_ref.shape[1] // PAIR
    head_ones = _head_ones()
    acc = _dot(onat_ref[...], wb_ref[...])
    pairs = range(n_pair)
    lanes = [slice(p * PAIR, (p + 1) * PAIR) for p in pairs]
    y = [yf_ref[:, l] + yb_ref[:, l] for l in lanes]
    mean = [_split_dot(a, head_ones, 2) * (1.0 / HEAD_DIM) for a in y]
    yc = [a - m for a, m in zip(y, mean)]
    var = [_split_dot(a * a, head_ones, 2) * (1.0 / HEAD_DIM) for a in yc]
    rk = [r_ref[:, l].astype(F32) * (0.5 * (kdf_ref[:, l].astype(F32) + kdb_ref[:, l].astype(F32)))
          * bonus_ref[:, l] for l in lanes]
    bonus = [_split_dot(a, head_ones, 2) for a in rk]
    cols = []
    for p in pairs:
        l = lanes[p]
        yn = yc[p] * lax.rsqrt(var[p] + GN_EPS) * gnw_ref[:, l] + gnb_ref[:, l]
        cols.append(((yn + bonus[p] * v_ref[:, l].astype(F32)) * g_ref[:, l].astype(F32)).astype(BF16))
    for q in range(0, n_pair, 2):
        acc = acc + _dot(jnp.concatenate(cols[q:q + 2], axis=1), wa_ref[q * PAIR:(q + 2) * PAIR, :])
    o_ref[...] = x_ref[...] + _rms(acc, gpost_ref[...])


def _mix_out(x, yf, yb, r, v, kdf, kdb, g, o_nat, bonus_scale, gn_w, gn_b, w_a, w_b, g_post):
    n, d = x.shape
    w = yf.shape[1]
    tm = _pick_tile(n, 256)
    tile = lambda cols: pl.BlockSpec((tm, cols), lambda i: (i, 0))
    row = lambda cols: pl.BlockSpec((1, cols), lambda i: (0, 0))
    resident = pl.BlockSpec((w, d), lambda i: (0, 0), pipeline_mode=pl.Buffered(1))
    return pl.pallas_call(
        _mix_out_kernel,
        grid=(n // tm,),
        in_specs=[tile(d)] + [tile(w)] * 8 + [row(w)] * 3 + [resident, resident, row(d)],
        out_specs=tile(d),
        out_shape=jax.ShapeDtypeStruct((n, d), F32),
        compiler_params=_params("parallel"),
    )(x, yf, yb, r, v, kdf, kdb, g, o_nat, bonus_scale, gn_w, gn_b, w_a, w_b, g_post)


def _encoder_layer(x, p, bias_tbl):
    b, t, d = x.shape
    n = b * t
    w = p['decay_up_fwd'].shape[1]
    x = x.reshape(n, d)
    x = _ffn(x, p['ffn1_pre_g'], p['ffn1_post_g'], p['ffn1_w_gate'], p['ffn1_w_up'], p['ffn1_w_down'])

    z_rwkv, qkv = _norm_proj(x, p['mix_pre_g'], p['w_in_rwkv'], p['w_in_nat'])

    r, v, kk, kdf, kdb, bf, bb, lwf, lwb, g = _rwkv_prep(
        z_rwkv, t, p['rwkv_shift_mix'],
        p['decay_bias_fwd'], p['decay_up_fwd'], p['decay_bias_bwd'], p['decay_up_bwd'],
        p['iclr_bias_fwd'], p['iclr_up_fwd'], p['iclr_bias_bwd'], p['iclr_up_bwd'], p['gate_up'],
        p['key_norm_scale'], p['key_iclr_mix'])
    seq = lambda a: a.reshape(b, t, w)
    yf = _wkv(seq(r), seq(v), seq(kk), seq(kdf), seq(bf), seq(lwf), False)
    yb = _wkv(seq(r), seq(v), seq(kk), seq(kdb), seq(bb), seq(lwb), True)
    o_nat = _natten(qkv.reshape(b, t, -1), bias_tbl).reshape(n, -1)

    x = _mix_out(x, yf.reshape(n, w), yb.reshape(n, w), r, v, kdf, kdb, g, o_nat,
                 p['bonus_scale'], p['gn_w'], p['gn_b'],
                 p['w_out_rwkv'], p['w_out_nat'], p['mix_post_g'])
    x = _ffn(x, p['ffn2_pre_g'], p['ffn2_post_g'], p['ffn2_w_gate'], p['ffn2_w_up'], p['ffn2_w_down'])
    return x.reshape(b, t, d)


_ROW_PARAMS = ('ffn1_pre_g', 'ffn1_post_g', 'mix_pre_g', 'rwkv_shift_mix',
               'decay_bias_fwd', 'decay_bias_bwd', 'iclr_bias_fwd', 'iclr_bias_bwd',
               'key_norm_scale', 'key_iclr_mix', 'bonus_scale', 'gn_w', 'gn_b',
               'mix_post_g', 'ffn2_pre_g', 'ffn2_post_g')
_BF16_MATS = ('ffn1_w_gate', 'ffn1_w_up', 'ffn1_w_down', 'ffn2_w_gate', 'ffn2_w_up', 'ffn2_w_down',
              'decay_up_fwd', 'decay_up_bwd', 'iclr_up_fwd', 'iclr_up_bwd', 'gate_up')


def _layer_params(weights, l):
    p = {name: weights[name][l].reshape(1, -1) for name in _ROW_PARAMS}
    p.update({name: weights[name][l].astype(BF16) for name in _BF16_MATS})
    w = weights['decay_up_fwd'].shape[-1]
    rwkv_cols = 3 * w + DECAY_RANK + ICLR_RANK + GATE_RANK
    w_in = weights['w_in'][l].astype(BF16)
    p['w_in_rwkv'] = w_in[:, :rwkv_cols]
    p['w_in_nat'] = w_in[:, rwkv_cols:]
    w_out = weights['w_out'][l].astype(BF16)
    p['w_out_rwkv'] = w_out[:w]
    p['w_out_nat'] = w_out[w:]
    return p, _nat_bias_table(weights['nat_rpb'][l])


def kernel(x_prompt, x_sample, ffn1_pre_g, ffn1_post_g, ffn1_w_gate, ffn1_w_up, ffn1_w_down, mix_pre_g, w_in, rwkv_shift_mix, decay_bias_fwd, decay_up_fwd, decay_bias_bwd, decay_up_bwd, iclr_bias_fwd, iclr_up_fwd, iclr_bias_bwd, iclr_up_bwd, gate_up, key_norm_scale, key_iclr_mix, bonus_scale, gn_w, gn_b, nat_rpb, w_out, mix_post_g, ffn2_pre_g, ffn2_post_g, ffn2_w_gate, ffn2_w_up, ffn2_w_down):
    weights = dict(
        ffn1_pre_g=ffn1_pre_g, ffn1_post_g=ffn1_post_g, ffn1_w_gate=ffn1_w_gate, ffn1_w_up=ffn1_w_up,
        ffn1_w_down=ffn1_w_down, mix_pre_g=mix_pre_g, w_in=w_in, rwkv_shift_mix=rwkv_shift_mix,
        decay_bias_fwd=decay_bias_fwd, decay_up_fwd=decay_up_fwd, decay_bias_bwd=decay_bias_bwd,
        decay_up_bwd=decay_up_bwd, iclr_bias_fwd=iclr_bias_fwd, iclr_up_fwd=iclr_up_fwd,
        iclr_bias_bwd=iclr_bias_bwd, iclr_up_bwd=iclr_up_bwd, gate_up=gate_up,
        key_norm_scale=key_norm_scale, key_iclr_mix=key_iclr_mix, bonus_scale=bonus_scale,
        gn_w=gn_w, gn_b=gn_b, nat_rpb=nat_rpb, w_out=w_out, mix_post_g=mix_post_g,
        ffn2_pre_g=ffn2_pre_g, ffn2_post_g=ffn2_post_g, ffn2_w_gate=ffn2_w_gate, ffn2_w_up=ffn2_w_up,
        ffn2_w_down=ffn2_w_down)
    depth = w_in.shape[0]
    layers = [_layer_params(weights, l) for l in range(depth)]
    outs = []
    for x in (x_prompt, x_sample):
        for p, bias_tbl in layers:
            x = _encoder_layer(x, p, bias_tbl)
        outs.append(x)
    return tuple(outs)
```

```python
import functools
import math

import jax
import jax.numpy as jnp
from jax import lax
from jax.experimental import pallas as pl
from jax.experimental.pallas import tpu as pltpu

F32 = jnp.float32
BF16 = jnp.bfloat16
HIGHEST = lax.Precision.HIGHEST

HEAD_DIM = 64
GRID_W = 64
WIN_H = 8
WIN_W = 16
DECAY_RANK = 64
ICLR_RANK = 64
GATE_RANK = 128
RMS_EPS = 1e-6
GN_EPS = 64e-5
DECAY_OFFSET = 0.5
FFN_RESIDUAL = 0.5
MASK_VALUE = -1e30

LANES = 128
HALO_ROWS = 16
EDGE_ROWS = 8
VMEM_LIMIT_BYTES = 56 * 1024 * 1024

PAIR = 2 * HEAD_DIM
CHUNK = 64
STACK = 2 * CHUNK
NAT_ROWS = 2
NAT_KEY_ROWS = WIN_H + NAT_ROWS - 1
NAT_VARIANTS = (((0, 0), (1, 0)), ((2, 0), (3, 0)), ((4, 0), (4, 1)), ((4, 1), (5, 1)), ((6, 1), (7, 1)))
WKV_BLOCK = 512
WKV_GROUP = 2


def _params(*semantics):
    return pltpu.CompilerParams(dimension_semantics=semantics, vmem_limit_bytes=VMEM_LIMIT_BYTES)


def _dot(a, b, precision=None):
    return jnp.dot(a, b, preferred_element_type=F32, precision=precision)


def _dot_nt(a, b):
    return lax.dot_general(a, b, (((1,), (1,)), ((), ())), preferred_element_type=F32)


def _dot_tn(a, b):
    return lax.dot_general(a, b, (((0,), (0,)), ((), ())), preferred_element_type=F32)


def _rms(x, g):
    return x * lax.rsqrt(jnp.mean(x * x, axis=-1, keepdims=True) + RMS_EPS) * g


def _sigmoid(x):
    return 0.5 * jnp.tanh(0.5 * x) + 0.5


def _head_ones():
    return jnp.where(
        (lax.broadcasted_iota(jnp.int32, (PAIR, PAIR), 0) // HEAD_DIM)
        == (lax.broadcasted_iota(jnp.int32, (PAIR, PAIR), 1) // HEAD_DIM), 1.0, 0.0).astype(BF16)


def _split_dot(x, w, terms, lhs=False):
    acc = None
    rem = x
    for t in range(terms):
        piece = rem.astype(BF16)
        d = _dot(w, piece) if lhs else _dot(piece, w)
        acc = d if acc is None else acc + d
        if t + 1 < terms:
            rem = rem - piece.astype(F32)
    return acc


def _pick_tile(n, target):
    t = min(n, target)
    while n % t:
        t //= 2
    return t


def _ffn_kernel(x_ref, gpre_ref, gpost_ref, wg_ref, wu_ref, wd_ref, o_ref, h_ref, acc_ref):
    j = pl.program_id(1)

    @pl.when(j == 0)
    def _():
        h_ref[...] = _rms(x_ref[...], gpre_ref[...]).astype(BF16)
        acc_ref[...] = jnp.zeros_like(acc_ref)

    h = h_ref[...]
    g = _dot(h, wg_ref[...])
    u = _dot(h, wu_ref[...])
    act = (g * _sigmoid(g) * u).astype(BF16)
    acc_ref[...] += _dot(act, wd_ref[...])

    @pl.when(j == pl.num_programs(1) - 1)
    def _():
        o_ref[...] = x_ref[...] + FFN_RESIDUAL * _rms(acc_ref[...], gpost_ref[...])


def _ffn(x, g_pre, g_post, w_gate, w_up, w_down):
    n, d = x.shape
    f = w_gate.shape[1]
    tm = _pick_tile(n, 512)
    tf = _pick_tile(f, 512)
    return pl.pallas_call(
        _ffn_kernel,
        grid=(n // tm, f // tf),
        in_specs=[
            pl.BlockSpec((tm, d), lambda i, j: (i, 0)),
            pl.BlockSpec((1, d), lambda i, j: (0, 0)),
            pl.BlockSpec((1, d), lambda i, j: (0, 0)),
            pl.BlockSpec((d, tf), lambda i, j: (0, j)),
            pl.BlockSpec((d, tf), lambda i, j: (0, j)),
            pl.BlockSpec((tf, d), lambda i, j: (j, 0)),
        ],
        out_specs=pl.BlockSpec((tm, d), lambda i, j: (i, 0)),
        out_shape=jax.ShapeDtypeStruct((n, d), F32),
        scratch_shapes=[pltpu.VMEM((tm, d), BF16), pltpu.VMEM((tm, d), F32)],
        compiler_params=_params("parallel", "arbitrary"),
    )(x, g_pre, g_post, w_gate, w_up, w_down)


def _norm_proj_kernel(x_ref, g_ref, wa_ref, wb_ref, oa_ref, ob_ref):
    h = _rms(x_ref[...], g_ref[...]).astype(BF16)
    oa_ref[...] = _dot(h, wa_ref[...]).astype(oa_ref.dtype)
    ob_ref[...] = _dot(h, wb_ref[...]).astype(ob_ref.dtype)


def _norm_proj(x, g, w_a, w_b):
    n, d = x.shape
    ca, cb = w_a.shape[1], w_b.shape[1]
    tm = _pick_tile(n, 256)
    resident = lambda cols: pl.BlockSpec((d, cols), lambda i: (0, 0), pipeline_mode=pl.Buffered(1))
    return pl.pallas_call(
        _norm_proj_kernel,
        grid=(n // tm,),
        in_specs=[
            pl.BlockSpec((tm, d), lambda i: (i, 0)),
            pl.BlockSpec((1, d), lambda i: (0, 0)),
            resident(ca),
            resident(cb),
        ],
        out_specs=[pl.BlockSpec((tm, ca), lambda i: (i, 0)), pl.BlockSpec((tm, cb), lambda i: (i, 0))],
        out_shape=[jax.ShapeDtypeStruct((n, ca), BF16), jax.ShapeDtypeStruct((n, cb), BF16)],
        compiler_params=_params("parallel"),
    )(x, g, w_a, w_b)


def _rwkv_prep_kernel(seq_len, z_ref, zp_ref, zn_ref, mu_ref,
                      dbf_ref, duf_ref, dbb_ref, dub_ref, ibf_ref, iuf_ref, ibb_ref, iub_ref, gu_ref,
                      kns_ref, mix_ref,
                      r_ref, v_ref, kk_ref, kdf_ref, kdb_ref, bf_ref, bb_ref, lwf_ref, lwb_ref, g_ref, u_ref):
    i = pl.program_id(0)
    tm = z_ref.shape[0]
    w = r_ref.shape[1]
    ti = lax.broadcasted_iota(jnp.int32, (tm, tm), 0)
    tj = lax.broadcasted_iota(jnp.int32, (tm, tm), 1)
    neighbours = jnp.where((ti == tj + 1) | (ti + 1 == tj), 1.0, 0.0).astype(BF16)
    z = z_ref[...]
    mu = mu_ref[...]
    u_ref[...] = (1.0 - mu) * z + (0.5 * mu) * _dot(neighbours, z)
    at_seq_start = (i * tm) % seq_len == 0
    at_seq_end = ((i + 1) * tm) % seq_len == 0
    prev_row = jnp.where(at_seq_start, 0.0, zp_ref[HALO_ROWS - 1:HALO_ROWS, :].astype(F32))
    next_row = jnp.where(at_seq_end, 0.0, zn_ref[0:1, :].astype(F32))
    edge = lax.broadcasted_iota(jnp.int32, (EDGE_ROWS, 1), 0)
    u_ref[0:EDGE_ROWS, :] += jnp.where(edge == 0, 0.5 * mu * prev_row, 0.0)
    u_ref[tm - EDGE_ROWS:tm, :] += jnp.where(edge == EDGE_ROWS - 1, 0.5 * mu * next_row, 0.0)

    r_ref[...] = u_ref[:, 0:w].astype(r_ref.dtype)
    v_ref[...] = u_ref[:, 2 * w:3 * w].astype(v_ref.dtype)
    o = 3 * w
    dec_lo = jnp.tanh(u_ref[:, o:o + DECAY_RANK]).astype(BF16)
    o += DECAY_RANK
    iclr_lo = u_ref[:, o:o + ICLR_RANK].astype(BF16)
    o += ICLR_RANK
    gate_lo = _sigmoid(u_ref[:, o:o + GATE_RANK]).astype(BF16)

    scale = -math.exp(-DECAY_OFFSET)
    lwf_ref[...] = scale * _sigmoid(dbf_ref[...] + _dot(dec_lo, duf_ref[...]))
    lwb_ref[...] = scale * _sigmoid(dbb_ref[...] + _dot(dec_lo, dub_ref[...]))
    g_ref[...] = _dot(gate_lo, gu_ref[...]).astype(g_ref.dtype)

    head_ones = _head_ones()
    for p in range(w // PAIR):
        lanes = slice(p * PAIR, (p + 1) * PAIR)
        k = u_ref[:, w + p * PAIR:w + (p + 1) * PAIR]
        kn = k * kns_ref[:, lanes]
        norm = jnp.sqrt(_split_dot(kn * kn, head_ones, 2))
        kk = kn / jnp.maximum(norm, 1e-12)
        kk_ref[:, lanes] = kk.astype(kk_ref.dtype)
        mix = mix_ref[:, lanes]
        for ib_ref, iu_ref, kd_ref, b_ref in ((ibf_ref, iuf_ref, kdf_ref, bf_ref),
                                              (ibb_ref, iub_ref, kdb_ref, bb_ref)):
            al = _sigmoid(ib_ref[:, lanes] + _dot(iclr_lo, iu_ref[:, lanes]))
            kd_ref[:, lanes] = (k * (1.0 + (al - 1.0) * mix)).astype(kd_ref.dtype)
            b_ref[:, lanes] = (kk * al).astype(b_ref.dtype)


def _rwkv_prep(z, seq_len, mu, dbf, duf, dbb, dub, ibf, iuf, ibb, iub, gate_up, kns, mix):
    n, c = z.shape
    assert z.dtype == BF16
    w = duf.shape[1]
    tm = _pick_tile(seq_len, 256)
    hb = tm // HALO_ROWS
    last_hb = n // HALO_ROWS - 1
    row_spec = lambda cols: pl.BlockSpec((1, cols), lambda i: (0, 0))
    mat_spec = lambda rows: pl.BlockSpec((rows, w), lambda i: (0, 0))
    out_spec = pl.BlockSpec((tm, w), lambda i: (i, 0))
    outs = [jax.ShapeDtypeStruct((n, w), dt) for dt in (BF16,) * 7 + (F32, F32, BF16)]
    return pl.pallas_call(
        functools.partial(_rwkv_prep_kernel, seq_len),
        grid=(n // tm,),
        in_specs=[
            pl.BlockSpec((tm, c), lambda i: (i, 0)),
            pl.BlockSpec((HALO_ROWS, c), lambda i: (jnp.maximum(i * hb - 1, 0), 0)),
            pl.BlockSpec((HALO_ROWS, c), lambda i: (jnp.minimum((i + 1) * hb, last_hb), 0)),
            row_spec(c),
            row_spec(w), mat_spec(DECAY_RANK), row_spec(w), mat_spec(DECAY_RANK),
            row_spec(w), mat_spec(ICLR_RANK), row_spec(w), mat_spec(ICLR_RANK),
            mat_spec(GATE_RANK), row_spec(w), row_spec(w),
        ],
        out_specs=[out_spec] * 10,
        out_shape=outs,
        scratch_shapes=[pltpu.VMEM((tm, c), F32)],
        compiler_params=_params("parallel"),
    )(z, z, z, mu, dbf, duf, dbb, dub, ibf, iuf, ibb, iub, gate_up, kns, mix)


def _stack(x, lane_lo):
    return jnp.concatenate([jnp.where(lane_lo, x, 0.0), jnp.where(lane_lo, 0.0, x)], axis=0)


def _wkv_kernel(reverse, r_ref, v_ref, kk_ref, kd_ref, b_ref, lw_ref, y_ref,
                s_ref, wr_s, u0_s, ar_s, vs_s, bk_s, dec_s):
    tc = pl.program_id(1)
    n_sub = r_ref.shape[0] // CHUNK
    n_pair = r_ref.shape[1] // PAIR

    @pl.when(tc == 0)
    def _():
        s_ref[...] = jnp.zeros_like(s_ref)

    ii = lax.broadcasted_iota(jnp.int32, (STACK, STACK), 0)
    jj = lax.broadcasted_iota(jnp.int32, (STACK, STACK), 1)
    ti = ii % CHUNK
    tj = jj % CHUNK
    same_head = (ii // CHUNK) == (jj // CHUNK)
    ii2 = lax.broadcasted_iota(jnp.int32, (STACK, 2 * STACK), 0)
    jj2 = lax.broadcasted_iota(jnp.int32, (STACK, 2 * STACK), 1)
    same_head2 = (ii2 // CHUNK) == ((jj2 % STACK) // CHUNK)
    if reverse:
        strict = same_head & (ti < tj)
        incl2 = same_head2 & (ii2 % CHUNK <= jj2 % CHUNK)
    else:
        strict = same_head & (ti > tj)
        incl2 = same_head2 & (ii2 % CHUNK >= jj2 % CHUNK)
    ci = lax.broadcasted_iota(jnp.int32, (CHUNK, CHUNK), 0)
    cj = lax.broadcasted_iota(jnp.int32, (CHUNK, CHUNK), 1)
    cum_tri = jnp.where((ci <= cj) if reverse else (ci >= cj), 1.0, 0.0).astype(BF16)
    eye = jnp.where(ii == jj, 1.0, 0.0).astype(F32)
    lane_lo = lax.broadcasted_iota(jnp.int32, (CHUNK, PAIR), 1) < HEAD_DIM
    end_row = 0 if reverse else CHUNK - 1

    level_masks = []
    m = 1
    while m < CHUNK:
        level_masks.append(strict & (ti // (2 * m) == tj // (2 * m)) & (ti // m != tj // m))
        m *= 2

    pairs = range(n_pair)
    lanes = [slice(p * PAIR, (p + 1) * PAIR) for p in pairs]

    def chunk_operands(g, carry):
        chunk_ids = [g * WKV_GROUP + q for q in range(WKV_GROUP)]
        chunk_rows = [pl.ds(pl.multiple_of(c * CHUNK, CHUNK), CHUNK) for c in chunk_ids]
        chunk_cum = [_split_dot(lw_ref[rows, :], cum_tri, 3, lhs=True) for rows in chunk_rows]
        chains = [(chunk_ids[q], p) for q in range(WKV_GROUP) for p in pairs]
        ops = []
        for i, (c, p) in enumerate(chains):
            q = i // n_pair
            rows = chunk_rows[q]
            r = r_ref[rows, lanes[p]].astype(F32)
            v = v_ref[rows, lanes[p]].astype(F32)
            kk = kk_ref[rows, lanes[p]].astype(F32)
            kd = kd_ref[rows, lanes[p]].astype(F32)
            b = b_ref[rows, lanes[p]].astype(F32)
            lw = lw_ref[rows, lanes[p]]

            cum = chunk_cum[q][:, lanes[p]]
            tot = cum[end_row:end_row + 1, :]
            e_in = jnp.exp(cum)
            e_ex = jnp.exp(cum - lw)
            e_inv = jnp.exp(-cum)
            e_end = jnp.exp(tot - cum)
            rt = _stack(r * e_in, lane_lo).astype(BF16)
            vs = _stack(v, lane_lo).astype(BF16)
            vs_s[c, p] = vs
            bk_s[c, p] = jnp.concatenate([_stack(b * e_end, lane_lo), _stack(kd * e_end, lane_lo)],
                                         axis=0).astype(BF16)
            dec_s[c, p] = jnp.exp(tot)
            ops.append(dict(
                rt=rt, vs=vs,
                at=_stack(-kk * e_ex, lane_lo).astype(BF16),
                bt=_stack(b * e_inv, lane_lo).astype(BF16),
                kt=_stack(kd * e_inv, lane_lo).astype(BF16)))

        m1 = [_dot_nt(jnp.concatenate([o['at'], o['rt']], axis=0),
                      jnp.concatenate([o['bt'], o['kt']], axis=0)) for o in ops]
        l_ab = [m[:STACK, :STACK] for m in m1]
        l_ak = [jnp.where(strict, m[:STACK, STACK:], 0.0).astype(BF16) for m in m1]
        for (c, p), m in zip(chains, m1):
            ar_s[c, p] = jnp.where(incl2, m[STACK:, :], 0.0).astype(BF16)

        inv = [eye + jnp.where(level_masks[0], l, 0.0) for l in l_ab]
        for mask in level_masks[1:]:
            inv_b = [x.astype(BF16) for x in inv]
            t = [_dot(jnp.where(mask, l, 0.0).astype(BF16), x).astype(BF16) for l, x in zip(l_ab, inv_b)]
            inv = [x + _dot(xb, y) for x, xb, y in zip(inv, inv_b, t)]
        inv_b = [x.astype(BF16) for x in inv]

        lakv = [_dot(l, o['vs']).astype(BF16) for l, o in zip(l_ak, ops)]
        for (c, p), x, o, y in zip(chains, inv_b, ops, lakv):
            wu = _dot(x, jnp.concatenate([o['at'], y], axis=1))
            wr_s[c, p] = jnp.concatenate([wu[:, :PAIR].astype(BF16), o['rt']], axis=0)
            u0_s[c, p] = wu[:, PAIR:]
        return carry

    lax.fori_loop(0, n_sub // WKV_GROUP, chunk_operands, 0)

    def advance(step, carry):
        c = (n_sub - 1 - step) if reverse else step
        rows = pl.ds(pl.multiple_of(c * CHUNK, CHUNK), CHUNK)
        s = [s_ref[p] for p in pairs]
        ur = [_dot_nt(wr_s[c, p], s[p].astype(BF16)) for p in pairs]
        uv = [jnp.concatenate([(ur[p][:STACK] + u0_s[c, p]).astype(BF16), vs_s[c, p]], axis=0)
              for p in pairs]
        for p in pairs:
            s_ref[p] = s[p] * dec_s[c, p] + _dot_tn(uv[p], bk_s[c, p])
        for p in pairs:
            ys = ur[p][STACK:] + _dot(ar_s[c, p], uv[p])
            y_ref[rows, lanes[p]] = ys[:CHUNK] + ys[CHUNK:]
        return carry

    lax.fori_loop(0, n_sub, advance, 0, unroll=True)


def _wkv(r, v, kk, kd, bb, lw, reverse):
    b, t, w = r.shape
    lc = _pick_tile(t, WKV_BLOCK)
    n_t = t // lc
    n_sub = lc // CHUNK
    n_pair = w // PAIR
    tmap = (lambda bi, ti: (bi, n_t - 1 - ti, 0)) if reverse else (lambda bi, ti: (bi, ti, 0))
    seq_spec = pl.BlockSpec((None, lc, w), tmap)
    per_chain = lambda rows, cols, dtype: pltpu.VMEM((n_sub, n_pair, rows, cols), dtype)
    return pl.pallas_call(
        functools.partial(_wkv_kernel, reverse),
        grid=(b, n_t),
        in_specs=[seq_spec] * 6,
        out_specs=seq_spec,
        out_shape=jax.ShapeDtypeStruct((b, t, w), F32),
        scratch_shapes=[
            pltpu.VMEM((n_pair, PAIR, PAIR), F32),
            per_chain(2 * STACK, PAIR, BF16),
            per_chain(STACK, PAIR, F32),
            per_chain(STACK, 2 * STACK, BF16),
            per_chain(STACK, PAIR, BF16),
            per_chain(2 * STACK, PAIR, BF16),
            per_chain(1, PAIR, F32),
        ],
        compiler_params=_params("parallel", "arbitrary"),
    )(r, v, kk, kd, bb, lw)


def _nat_bias_table(rpb):
    off = jnp.arange(WIN_H)
    row_rel = off[None, :] - off[:, None] + (WIN_H - 1)
    cols = jnp.arange(GRID_W)
    col_start = jnp.clip(cols - WIN_W // 2, 0, GRID_W - WIN_W)
    in_win = (cols[None, :] >= col_start[:, None]) & (cols[None, :] < col_start[:, None] + WIN_W)
    col_rel = cols[None, :] - cols[:, None] + (WIN_W - 1)
    pick = (in_win[:, :, None] & (col_rel[:, :, None] == jnp.arange(2 * WIN_W - 1))).astype(F32)
    tbl = jnp.einsum('hdar,jcr->hdjac', rpb.astype(F32)[:, row_rel], pick, precision=HIGHEST)
    tbl = jnp.where(in_win[None, None, :, None, :], tbl, MASK_VALUE)

    def pair_table(variant):
        rows = [jnp.pad(tbl[:, d], ((0, 0), (0, 0), (off, NAT_KEY_ROWS - WIN_H - off), (0, 0)),
                        constant_values=MASK_VALUE) for d, off in variant]
        return jnp.stack(rows, axis=1).reshape(rpb.shape[0], NAT_ROWS * GRID_W, NAT_KEY_ROWS * GRID_W)

    return jnp.stack([pair_table(v) for v in NAT_VARIANTS], axis=1)


def _nat_row_start(i, n_rows):
    return jnp.clip(i - WIN_H // 2, 0, n_rows - WIN_H)


def _nat_key_start(g, n_rows):
    return jnp.minimum(_nat_row_start(NAT_ROWS * g, n_rows), n_rows - NAT_KEY_ROWS)


def _nat_variant(g, n_groups):
    return jnp.where(g < 2, g, jnp.where(g >= n_groups - 2, g - (n_groups - 5), 2))


def _natten_kernel(n_rows, q_ref, k_ref, v_ref, bias_ref, o_ref):
    g = pl.program_id(1)
    n_pair = q_ref.shape[1] // PAIR
    n_q = NAT_ROWS * GRID_W
    n_keys = NAT_KEY_ROWS * GRID_W
    key_rows = pl.ds(pl.multiple_of(_nat_key_start(g, n_rows) * GRID_W, GRID_W), n_keys)
    lane_lo = lax.broadcasted_iota(jnp.int32, (n_q, PAIR), 1) < HEAD_DIM
    scale = HEAD_DIM ** -0.5
    heads = range(2 * n_pair)
    lanes = [slice((h // 2) * PAIR, (h // 2 + 1) * PAIR) for h in heads]
    s = []
    for h in heads:
        q = q_ref[:, lanes[h]] * scale
        q_h = jnp.where(lane_lo if h % 2 == 0 else ~lane_lo, q, jnp.zeros_like(q))
        s.append(_dot_nt(q_h, k_ref[key_rows, lanes[h]]) + bias_ref[h, 0])
    e = [jnp.exp(x - jnp.max(x, axis=-1, keepdims=True)) for x in s]
    denom = [jnp.sum(x, axis=-1, keepdims=True) for x in e]
    outs = [_dot(e[h].astype(BF16), v_ref[key_rows, lanes[h]]) / denom[h] for h in heads]
    for p in range(n_pair):
        o_ref[:, lanes[2 * p]] = jnp.where(lane_lo, outs[2 * p], outs[2 * p + 1]).astype(o_ref.dtype)


def _natten(qkv, bias_tbl):
    b, t, w3 = qkv.shape
    w = w3 // 3
    n_rows = t // GRID_W
    n_groups = n_rows // NAT_ROWS
    n_heads = w // HEAD_DIM
    assert n_rows % NAT_ROWS == 0 and n_groups >= 5
    n_q = NAT_ROWS * GRID_W
    bias_map = lambda bi, g: (0, _nat_variant(g, n_groups), 0, 0)
    return pl.pallas_call(
        functools.partial(_natten_kernel, n_rows),
        grid=(b, n_groups),
        in_specs=[
            pl.BlockSpec((None, n_q, w), lambda bi, g: (bi, g, 0)),
            pl.BlockSpec((None, t, w), lambda bi, g: (bi, 0, 1)),
            pl.BlockSpec((None, t, w), lambda bi, g: (bi, 0, 2)),
            pl.BlockSpec((n_heads, 1, n_q, NAT_KEY_ROWS * GRID_W), bias_map),
        ],
        out_specs=pl.BlockSpec((None, n_q, w), lambda bi, g: (bi, g, 0)),
        out_shape=jax.ShapeDtypeStruct((b, t, w), BF16),
        compiler_params=_params("parallel", "arbitrary"),
    )(qkv, qkv, qkv, bias_tbl)


def _mix_out_kernel(x_ref, yf_ref, yb_ref, r_ref, v_ref, kdf_ref, kdb_ref, g_ref, onat_ref,
                    bonus_ref, gnw_ref, gnb_ref, wa_ref, wb_ref, gpost_ref, o_ref):
    n_pair = yf_ref.shape[1] // PAIR
    head_ones = _head_ones()
    acc = _dot(onat_ref[...], wb_ref[...])
    pairs = range(n_pair)
    lanes = [slice(p * PAIR, (p + 1) * PAIR) for p in pairs]
    y = [yf_ref[:, l] + yb_ref[:, l] for l in lanes]
    mean = [_split_dot(a, head_ones, 2) * (1.0 / HEAD_DIM) for a in y]
    yc = [a - m for a, m in zip(y, mean)]
    var = [_split_dot(a * a, head_ones, 2) * (1.0 / HEAD_DIM) for a in yc]
    rk = [r_ref[:, l].astype(F32) * (0.5 * (kdf_ref[:, l].astype(F32) + kdb_ref[:, l].astype(F32)))
          * bonus_ref[:, l] for l in lanes]
    bonus = [_split_dot(a, head_ones, 2) for a in rk]
    cols = []
    for p in pairs:
        l = lanes[p]
        yn = yc[p] * lax.rsqrt(var[p] + GN_EPS) * gnw_ref[:, l] + gnb_ref[:, l]
        cols.append(((yn + bonus[p] * v_ref[:, l].astype(F32)) * g_ref[:, l].astype(F32)).astype(BF16))
    for q in range(0, n_pair, 2):
        acc = acc + _dot(jnp.concatenate(cols[q:q + 2], axis=1), wa_ref[q * PAIR:(q + 2) * PAIR, :])
    o_ref[...] = x_ref[...] + _rms(acc, gpost_ref[...])


def _mix_out(x, yf, yb, r, v, kdf, kdb, g, o_nat, bonus_scale, gn_w, gn_b, w_a, w_b, g_post):
    n, d = x.shape
    w = yf.shape[1]
    tm = _pick_tile(n, 256)
    tile = lambda cols: pl.BlockSpec((tm, cols), lambda i: (i, 0))
    row = lambda cols: pl.BlockSpec((1, cols), lambda i: (0, 0))
    resident = pl.BlockSpec((w, d), lambda i: (0, 0), pipeline_mode=pl.Buffered(1))
    return pl.pallas_call(
        _mix_out_kernel,
        grid=(n // tm,),
        in_specs=[tile(d)] + [tile(w)] * 8 + [row(w)] * 3 + [resident, resident, row(d)],
        out_specs=tile(d),
        out_shape=jax.ShapeDtypeStruct((n, d), F32),
        compiler_params=_params("parallel"),
    )(x, yf, yb, r, v, kdf, kdb, g, o_nat, bonus_scale, gn_w, gn_b, w_a, w_b, g_post)


def _encoder_layer(x, p, bias_tbl):
    b, t, d = x.shape
    n = b * t
    w = p['decay_up_fwd'].shape[1]
    x = x.reshape(n, d)
    x = _ffn(x, p['ffn1_pre_g'], p['ffn1_post_g'], p['ffn1_w_gate'], p['ffn1_w_up'], p['ffn1_w_down'])

    z_rwkv, qkv = _norm_proj(x, p['mix_pre_g'], p['w_in_rwkv'], p['w_in_nat'])

    r, v, kk, kdf, kdb, bf, bb, lwf, lwb, g = _rwkv_prep(
        z_rwkv, t, p['rwkv_shift_mix'],
        p['decay_bias_fwd'], p['decay_up_fwd'], p['decay_bias_bwd'], p['decay_up_bwd'],
        p['iclr_bias_fwd'], p['iclr_up_fwd'], p['iclr_bias_bwd'], p['iclr_up_bwd'], p['gate_up'],
        p['key_norm_scale'], p['key_iclr_mix'])
    seq = lambda a: a.reshape(b, t, w)
    yf = _wkv(seq(r), seq(v), seq(kk), seq(kdf), seq(bf), seq(lwf), False)
    yb = _wkv(seq(r), seq(v), seq(kk), seq(kdb), seq(bb), seq(lwb), True)
    o_nat = _natten(qkv.reshape(b, t, -1), bias_tbl).reshape(n, -1)

    x = _mix_out(x, yf.reshape(n, w), yb.reshape(n, w), r, v, kdf, kdb, g, o_nat,
                 p['bonus_scale'], p['gn_w'], p['gn_b'],
                 p['w_out_rwkv'], p['w_out_nat'], p['mix_post_g'])
    x = _ffn(x, p['ffn2_pre_g'], p['ffn2_post_g'], p['ffn2_w_gate'], p['ffn2_w_up'], p['ffn2_w_down'])
    return x.reshape(b, t, d)


_ROW_PARAMS = ('ffn1_pre_g', 'ffn1_post_g', 'mix_pre_g', 'rwkv_shift_mix',
               'decay_bias_fwd', 'decay_bias_bwd', 'iclr_bias_fwd', 'iclr_bias_bwd',
               'key_norm_scale', 'key_iclr_mix', 'bonus_scale', 'gn_w', 'gn_b',
               'mix_post_g', 'ffn2_pre_g', 'ffn2_post_g')
_BF16_MATS = ('ffn1_w_gate', 'ffn1_w_up', 'ffn1_w_down', 'ffn2_w_gate', 'ffn2_w_up', 'ffn2_w_down',
              'decay_up_fwd', 'decay_up_bwd', 'iclr_up_fwd', 'iclr_up_bwd', 'gate_up')


def _layer_params(weights, l):
    p = {name: weights[name][l].reshape(1, -1) for name in _ROW_PARAMS}
    p.update({name: weights[name][l].astype(BF16) for name in _BF16_MATS})
    w = weights['decay_up_fwd'].shape[-1]
    rwkv_cols = 3 * w + DECAY_RANK + ICLR_RANK + GATE_RANK
    w_in = weights['w_in'][l].astype(BF16)
    p['w_in_rwkv'] = w_in[:, :rwkv_cols]
    p['w_in_nat'] = w_in[:, rwkv_cols:]
    w_out = weights['w_out'][l].astype(BF16)
    p['w_out_rwkv'] = w_out[:w]
    p['w_out_nat'] = w_out[w:]
    return p, _nat_bias_table(weights['nat_rpb'][l])


def kernel(x_prompt, x_sample, ffn1_pre_g, ffn1_post_g, ffn1_w_gate, ffn1_w_up, ffn1_w_down, mix_pre_g, w_in, rwkv_shift_mix, decay_bias_fwd, decay_up_fwd, decay_bias_bwd, decay_up_bwd, iclr_bias_fwd, iclr_up_fwd, iclr_bias_bwd, iclr_up_bwd, gate_up, key_norm_scale, key_iclr_mix, bonus_scale, gn_w, gn_b, nat_rpb, w_out, mix_post_g, ffn2_pre_g, ffn2_post_g, ffn2_w_gate, ffn2_w_up, ffn2_w_down):
    weights = dict(
        ffn1_pre_g=ffn1_pre_g, ffn1_post_g=ffn1_post_g, ffn1_w_gate=ffn1_w_gate, ffn1_w_up=ffn1_w_up,
        ffn1_w_down=ffn1_w_down, mix_pre_g=mix_pre_g, w_in=w_in, rwkv_shift_mix=rwkv_shift_mix,
        decay_bias_fwd=decay_bias_fwd, decay_up_fwd=decay_up_fwd, decay_bias_bwd=decay_bias_bwd,
        decay_up_bwd=decay_up_bwd, iclr_bias_fwd=iclr_bias_fwd, iclr_up_fwd=iclr_up_fwd,
        iclr_bias_bwd=iclr_bias_bwd, iclr_up_bwd=iclr_up_bwd, gate_up=gate_up,
        key_norm_scale=key_norm_scale, key_iclr_mix=key_iclr_mix, bonus_scale=bonus_scale,
        gn_w=gn_w, gn_b=gn_b, nat_rpb=nat_rpb, w_out=w_out, mix_post_g=mix_post_g,
        ffn2_pre_g=ffn2_pre_g, ffn2_post_g=ffn2_post_g, ffn2_w_gate=ffn2_w_gate, ffn2_w_up=ffn2_w_up,
        ffn2_w_down=ffn2_w_down)
    depth = w_in.shape[0]
    layers = [_layer_params(weights, l) for l in range(depth)]
    outs = []
    for x in (x_prompt, x_sample):
        for p, bias_tbl in layers:
            x = _encoder_layer(x, p, bias_tbl)
        outs.append(x)
    return tuple(outs)
```

```python
import functools
import math

import jax
import jax.numpy as jnp
from jax import lax
from jax.experimental import pallas as pl
from jax.experimental.pallas import tpu as pltpu

F32 = jnp.float32
BF16 = jnp.bfloat16
HIGHEST = lax.Precision.HIGHEST

HEAD_DIM = 64
GRID_W = 64
WIN_H = 8
WIN_W = 16
DECAY_RANK = 64
ICLR_RANK = 64
GATE_RANK = 128
RMS_EPS = 1e-6
GN_EPS = 64e-5
DECAY_OFFSET = 0.5
FFN_RESIDUAL = 0.5
MASK_VALUE = -1e30

LANES = 128
HALO_ROWS = 16
EDGE_ROWS = 8
VMEM_LIMIT_BYTES = 56 * 1024 * 1024

PAIR = 2 * HEAD_DIM
CHUNK = 64
STACK = 2 * CHUNK
NAT_ROWS = 2
NAT_KEY_ROWS = WIN_H + NAT_ROWS - 1
NAT_VARIANTS = (((0, 0), (1, 0)), ((2, 0), (3, 0)), ((4, 0), (4, 1)), ((4, 1), (5, 1)), ((6, 1), (7, 1)))
WKV_BLOCK = 512
WKV_GROUP = 2


def _params(*semantics):
    return pltpu.CompilerParams(dimension_semantics=semantics, vmem_limit_bytes=VMEM_LIMIT_BYTES)


def _dot(a, b, precision=None):
    return jnp.dot(a, b, preferred_element_type=F32, precision=precision)


def _dot_nt(a, b):
    return lax.dot_general(a, b, (((1,), (1,)), ((), ())), preferred_element_type=F32)


def _dot_tn(a, b):
    return lax.dot_general(a, b, (((0,), (0,)), ((), ())), preferred_element_type=F32)


def _rms(x, g):
    return x * lax.rsqrt(jnp.mean(x * x, axis=-1, keepdims=True) + RMS_EPS) * g


def _sigmoid(x):
    return 0.5 * jnp.tanh(0.5 * x) + 0.5


def _head_ones():
    return jnp.where(
        (lax.broadcasted_iota(jnp.int32, (PAIR, PAIR), 0) // HEAD_DIM)
        == (lax.broadcasted_iota(jnp.int32, (PAIR, PAIR), 1) // HEAD_DIM), 1.0, 0.0).astype(BF16)


def _split_dot(x, w, terms, lhs=False):
    acc = None
    rem = x
    for t in range(terms):
        piece = rem.astype(BF16)
        d = _dot(w, piece) if lhs else _dot(piece, w)
        acc = d if acc is None else acc + d
        if t + 1 < terms:
            rem = rem - piece.astype(F32)
    return acc


def _pick_tile(n, target):
    t = min(n, target)
    while n % t:
        t //= 2
    return t


def _ffn_kernel(x_ref, gpre_ref, gpost_ref, wg_ref, wu_ref, wd_ref, o_ref, h_ref, acc_ref):
    j = pl.program_id(1)

    @pl.when(j == 0)
    def _():
        h_ref[...] = _rms(x_ref[...], gpre_ref[...]).astype(BF16)
        acc_ref[...] = jnp.zeros_like(acc_ref)

    h = h_ref[...]
    g = _dot(h, wg_ref[...])
    u = _dot(h, wu_ref[...])
    act = (g * _sigmoid(g) * u).astype(BF16)
    acc_ref[...] += _dot(act, wd_ref[...])

    @pl.when(j == pl.num_programs(1) - 1)
    def _():
        o_ref[...] = x_ref[...] + FFN_RESIDUAL * _rms(acc_ref[...], gpost_ref[...])


def _ffn(x, g_pre, g_post, w_gate, w_up, w_down):
    n, d = x.shape
    f = w_gate.shape[1]
    tm = _pick_tile(n, 512)
    tf = _pick_tile(f, 512)
    return pl.pallas_call(
        _ffn_kernel,
        grid=(n // tm, f // tf),
        in_specs=[
            pl.BlockSpec((tm, d), lambda i, j: (i, 0)),
            pl.BlockSpec((1, d), lambda i, j: (0, 0)),
            pl.BlockSpec((1, d), lambda i, j: (0, 0)),
            pl.BlockSpec((d, tf), lambda i, j: (0, j)),
            pl.BlockSpec((d, tf), lambda i, j: (0, j)),
            pl.BlockSpec((tf, d), lambda i, j: (j, 0)),
        ],
        out_specs=pl.BlockSpec((tm, d), lambda i, j: (i, 0)),
        out_shape=jax.ShapeDtypeStruct((n, d), F32),
        scratch_shapes=[pltpu.VMEM((tm, d), BF16), pltpu.VMEM((tm, d), F32)],
        compiler_params=_params("parallel", "arbitrary"),
    )(x, g_pre, g_post, w_gate, w_up, w_down)


def _norm_proj_kernel(x_ref, g_ref, wa_ref, wb_ref, oa_ref, ob_ref):
    h = _rms(x_ref[...], g_ref[...]).astype(BF16)
    oa_ref[...] = _dot(h, wa_ref[...]).astype(oa_ref.dtype)
    ob_ref[...] = _dot(h, wb_ref[...]).astype(ob_ref.dtype)


def _norm_proj(x, g, w_a, w_b):
    n, d = x.shape
    ca, cb = w_a.shape[1], w_b.shape[1]
    tm = _pick_tile(n, 256)
    resident = lambda cols: pl.BlockSpec((d, cols), lambda i: (0, 0), pipeline_mode=pl.Buffered(1))
    return pl.pallas_call(
        _norm_proj_kernel,
        grid=(n // tm,),
        in_specs=[
            pl.BlockSpec((tm, d), lambda i: (i, 0)),
            pl.BlockSpec((1, d), lambda i: (0, 0)),
            resident(ca),
            resident(cb),
        ],
        out_specs=[pl.BlockSpec((tm, ca), lambda i: (i, 0)), pl.BlockSpec((tm, cb), lambda i: (i, 0))],
        out_shape=[jax.ShapeDtypeStruct((n, ca), BF16), jax.ShapeDtypeStruct((n, cb), BF16)],
        compiler_params=_params("parallel"),
    )(x, g, w_a, w_b)


def _rwkv_prep_kernel(seq_len, z_ref, zp_ref, zn_ref, mu_ref,
                      dbf_ref, duf_ref, dbb_ref, dub_ref, ibf_ref, iuf_ref, ibb_ref, iub_ref, gu_ref,
                      kns_ref, mix_ref,
                      r_ref, v_ref, kk_ref, kdf_ref, kdb_ref, bf_ref, bb_ref, lwf_ref, lwb_ref, g_ref, u_ref):
    i = pl.program_id(0)
    tm = z_ref.shape[0]
    w = r_ref.shape[1]
    ti = lax.broadcasted_iota(jnp.int32, (tm, tm), 0)
    tj = lax.broadcasted_iota(jnp.int32, (tm, tm), 1)
    neighbours = jnp.where((ti == tj + 1) | (ti + 1 == tj), 1.0, 0.0).astype(BF16)
    z = z_ref[...]
    mu = mu_ref[...]
    u_ref[...] = (1.0 - mu) * z + (0.5 * mu) * _dot(neighbours, z)
    at_seq_start = (i * tm) % seq_len == 0
    at_seq_end = ((i + 1) * tm) % seq_len == 0
    prev_row = jnp.where(at_seq_start, 0.0, zp_ref[HALO_ROWS - 1:HALO_ROWS, :].astype(F32))
    next_row = jnp.where(at_seq_end, 0.0, zn_ref[0:1, :].astype(F32))
    edge = lax.broadcasted_iota(jnp.int32, (EDGE_ROWS, 1), 0)
    u_ref[0:EDGE_ROWS, :] += jnp.where(edge == 0, 0.5 * mu * prev_row, 0.0)
    u_ref[tm - EDGE_ROWS:tm, :] += jnp.where(edge == EDGE_ROWS - 1, 0.5 * mu * next_row, 0.0)

    r_ref[...] = u_ref[:, 0:w].astype(r_ref.dtype)
    v_ref[...] = u_ref[:, 2 * w:3 * w].astype(v_ref.dtype)
    o = 3 * w
    dec_lo = jnp.tanh(u_ref[:, o:o + DECAY_RANK]).astype(BF16)
    o += DECAY_RANK
    iclr_lo = u_ref[:, o:o + ICLR_RANK].astype(BF16)
    o += ICLR_RANK
    gate_lo = _sigmoid(u_ref[:, o:o + GATE_RANK]).astype(BF16)

    scale = -math.exp(-DECAY_OFFSET)
    lwf_ref[...] = scale * _sigmoid(dbf_ref[...] + _dot(dec_lo, duf_ref[...]))
    lwb_ref[...] = scale * _sigmoid(dbb_ref[...] + _dot(dec_lo, dub_ref[...]))
    g_ref[...] = _dot(gate_lo, gu_ref[...]).astype(g_ref.dtype)

    head_ones = _head_ones()
    for p in range(w // PAIR):
        lanes = slice(p * PAIR, (p + 1) * PAIR)
        k = u_ref[:, w + p * PAIR:w + (p + 1) * PAIR]
        kn = k * kns_ref[:, lanes]
        norm = jnp.sqrt(_split_dot(kn * kn, head_ones, 2))
        kk = kn / jnp.maximum(norm, 1e-12)
        kk_ref[:, lanes] = kk.astype(kk_ref.dtype)
        mix = mix_ref[:, lanes]
        for ib_ref, iu_ref, kd_ref, b_ref in ((ibf_ref, iuf_ref, kdf_ref, bf_ref),
                                              (ibb_ref, iub_ref, kdb_ref, bb_ref)):
            al = _sigmoid(ib_ref[:, lanes] + _dot(iclr_lo, iu_ref[:, lanes]))
            kd_ref[:, lanes] = (k * (1.0 + (al - 1.0) * mix)).astype(kd_ref.dtype)
            b_ref[:, lanes] = (kk * al).astype(b_ref.dtype)


def _rwkv_prep(z, seq_len, mu, dbf, duf, dbb, dub, ibf, iuf, ibb, iub, gate_up, kns, mix):
    n, c = z.shape
    assert z.dtype == BF16
    w = duf.shape[1]
    tm = _pick_tile(seq_len, 256)
    hb = tm // HALO_ROWS
    last_hb = n // HALO_ROWS - 1
    row_spec = lambda cols: pl.BlockSpec((1, cols), lambda i: (0, 0))
    mat_spec = lambda rows: pl.BlockSpec((rows, w), lambda i: (0, 0))
    out_spec = pl.BlockSpec((tm, w), lambda i: (i, 0))
    outs = [jax.ShapeDtypeStruct((n, w), dt) for dt in (BF16,) * 7 + (F32, F32, BF16)]
    return pl.pallas_call(
        functools.partial(_rwkv_prep_kernel, seq_len),
        grid=(n // tm,),
        in_specs=[
            pl.BlockSpec((tm, c), lambda i: (i, 0)),
            pl.BlockSpec((HALO_ROWS, c), lambda i: (jnp.maximum(i * hb - 1, 0), 0)),
            pl.BlockSpec((HALO_ROWS, c), lambda i: (jnp.minimum((i + 1) * hb, last_hb), 0)),
            row_spec(c),
            row_spec(w), mat_spec(DECAY_RANK), row_spec(w), mat_spec(DECAY_RANK),
            row_spec(w), mat_spec(ICLR_RANK), row_spec(w), mat_spec(ICLR_RANK),
            mat_spec(GATE_RANK), row_spec(w), row_spec(w),
        ],
        out_specs=[out_spec] * 10,
        out_shape=outs,
        scratch_shapes=[pltpu.VMEM((tm, c), F32)],
        compiler_params=_params("parallel"),
    )(z, z, z, mu, dbf, duf, dbb, dub, ibf, iuf, ibb, iub, gate_up, kns, mix)


def _stack(x, lane_lo):
    return jnp.concatenate([jnp.where(lane_lo, x, 0.0), jnp.where(lane_lo, 0.0, x)], axis=0)


def _wkv_kernel(reverse, r_ref, v_ref, kk_ref, kd_ref, b_ref, lw_ref, y_ref,
                s_ref, wr_s, u0_s, ar_s, vs_s, bk_s, dec_s):
    tc = pl.program_id(1)
    n_sub = r_ref.shape[0] // CHUNK
    n_pair = r_ref.shape[1] // PAIR

    @pl.when(tc == 0)
    def _():
        s_ref[...] = jnp.zeros_like(s_ref)

    ii = lax.broadcasted_iota(jnp.int32, (STACK, STACK), 0)
    jj = lax.broadcasted_iota(jnp.int32, (STACK, STACK), 1)
    ti = ii % CHUNK
    tj = jj % CHUNK
    same_head = (ii // CHUNK) == (jj // CHUNK)
    ii2 = lax.broadcasted_iota(jnp.int32, (STACK, 2 * STACK), 0)
    jj2 = lax.broadcasted_iota(jnp.int32, (STACK, 2 * STACK), 1)
    same_head2 = (ii2 // CHUNK) == ((jj2 % STACK) // CHUNK)
    if reverse:
        strict = same_head & (ti < tj)
        incl2 = same_head2 & (ii2 % CHUNK <= jj2 % CHUNK)
    else:
        strict = same_head & (ti > tj)
        incl2 = same_head2 & (ii2 % CHUNK >= jj2 % CHUNK)
    ci = lax.broadcasted_iota(jnp.int32, (CHUNK, CHUNK), 0)
    cj = lax.broadcasted_iota(jnp.int32, (CHUNK, CHUNK), 1)
    cum_tri = jnp.where((ci <= cj) if reverse else (ci >= cj), 1.0, 0.0).astype(BF16)
    eye = jnp.where(ii == jj, 1.0, 0.0).astype(F32)
    lane_lo = lax.broadcasted_iota(jnp.int32, (CHUNK, PAIR), 1) < HEAD_DIM
    end_row = 0 if reverse else CHUNK - 1

    level_masks = []
    m = 1
    while m < CHUNK:
        level_masks.append(strict & (ti // (2 * m) == tj // (2 * m)) & (ti // m != tj // m))
        m *= 2

    pairs = range(n_pair)
    lanes = [slice(p * PAIR, (p + 1) * PAIR) for p in pairs]

    def chunk_operands(g, carry):
        chunk_ids = [g * WKV_GROUP + q for q in range(WKV_GROUP)]
        chunk_rows = [pl.ds(pl.multiple_of(c * CHUNK, CHUNK), CHUNK) for c in chunk_ids]
        chunk_cum = [_split_dot(lw_ref[rows, :], cum_tri, 3, lhs=True) for rows in chunk_rows]
        chains = [(chunk_ids[q], p) for q in range(WKV_GROUP) for p in pairs]
        ops = []
        for i, (c, p) in enumerate(chains):
            q = i // n_pair
            rows = chunk_rows[q]
            r = r_ref[rows, lanes[p]].astype(F32)
            v = v_ref[rows, lanes[p]].astype(F32)
            kk = kk_ref[rows, lanes[p]].astype(F32)
            kd = kd_ref[rows, lanes[p]].astype(F32)
            b = b_ref[rows, lanes[p]].astype(F32)
            lw = lw_ref[rows, lanes[p]]

            cum = chunk_cum[q][:, lanes[p]]
            tot = cum[end_row:end_row + 1, :]
            e_in = jnp.exp(cum)
            e_ex = jnp.exp(cum - lw)
            e_inv = jnp.exp(-cum)
            e_end = jnp.exp(tot - cum)
            rt = _stack(r * e_in, lane_lo).astype(BF16)
            vs = _stack(v, lane_lo).astype(BF16)
            vs_s[c, p] = vs
            bk_s[c, p] = jnp.concatenate([_stack(b * e_end, lane_lo), _stack(kd * e_end, lane_lo)],
                                         axis=0).astype(BF16)
            dec_s[c, p] = jnp.exp(tot)
            ops.append(dict(
                rt=rt, vs=vs,
                at=_stack(-kk * e_ex, lane_lo).astype(BF16),
                bt=_stack(b * e_inv, lane_lo).astype(BF16),
                kt=_stack(kd * e_inv, lane_lo).astype(BF16)))

        m1 = [_dot_nt(jnp.concatenate([o['at'], o['rt']], axis=0),
                      jnp.concatenate([o['bt'], o['kt']], axis=0)) for o in ops]
        l_ab = [m[:STACK, :STACK] for m in m1]
        l_ak = [jnp.where(strict, m[:STACK, STACK:], 0.0).astype(BF16) for m in m1]
        for (c, p), m in zip(chains, m1):
            ar_s[c, p] = jnp.where(incl2, m[STACK:, :], 0.0).astype(BF16)

        inv = [eye + jnp.where(level_masks[0], l, 0.0) for l in l_ab]
        for mask in level_masks[1:]:
            inv_b = [x.astype(BF16) for x in inv]
            t = [_dot(jnp.where(mask, l, 0.0).astype(BF16), x).astype(BF16) for l, x in zip(l_ab, inv_b)]
            inv = [x + _dot(xb, y) for x, xb, y in zip(inv, inv_b, t)]
        inv_b = [x.astype(BF16) for x in inv]

        lakv = [_dot(l, o['vs']).astype(BF16) for l, o in zip(l_ak, ops)]
        for (c, p), x, o, y in zip(chains, inv_b, ops, lakv):
            wu = _dot(x, jnp.concatenate([o['at'], y], axis=1))
            wr_s[c, p] = jnp.concatenate([wu[:, :PAIR].astype(BF16), o['rt']], axis=0)
            u0_s[c, p] = wu[:, PAIR:]
        return carry

    lax.fori_loop(0, n_sub // WKV_GROUP, chunk_operands, 0)

    def advance(step, carry):
        c = (n_sub - 1 - step) if reverse else step
        rows = pl.ds(pl.multiple_of(c * CHUNK, CHUNK), CHUNK)
        s = [s_ref[p] for p in pairs]
        ur = [_dot_nt(wr_s[c, p], s[p].astype(BF16)) for p in pairs]
        uv = [jnp.concatenate([(ur[p][:STACK] + u0_s[c, p]).astype(BF16), vs_s[c, p]], axis=0)
              for p in pairs]
        for p in pairs:
            s_ref[p] = s[p] * dec_s[c, p] + _dot_tn(uv[p], bk_s[c, p])
        for p in pairs:
            ys = ur[p][STACK:] + _dot(ar_s[c, p], uv[p])
            y_ref[rows, lanes[p]] = ys[:CHUNK] + ys[CHUNK:]
        return carry

    lax.fori_loop(0, n_sub, advance, 0, unroll=True)


def _wkv(r, v, kk, kd, bb, lw, reverse):
    b, t, w = r.shape
    lc = _pick_tile(t, WKV_BLOCK)
    n_t = t // lc
    n_sub = lc // CHUNK
    n_pair = w // PAIR
    tmap = (lambda bi, ti: (bi, n_t - 1 - ti, 0)) if reverse else (lambda bi, ti: (bi, ti, 0))
    seq_spec = pl.BlockSpec((None, lc, w), tmap)
    per_chain = lambda rows, cols, dtype: pltpu.VMEM((n_sub, n_pair, rows, cols), dtype)
    return pl.pallas_call(
        functools.partial(_wkv_kernel, reverse),
        grid=(b, n_t),
        in_specs=[seq_spec] * 6,
        out_specs=seq_spec,
        out_shape=jax.ShapeDtypeStruct((b, t, w), F32),
        scratch_shapes=[
            pltpu.VMEM((n_pair, PAIR, PAIR), F32),
            per_chain(2 * STACK, PAIR, BF16),
            per_chain(STACK, PAIR, F32),
            per_chain(STACK, 2 * STACK, BF16),
            per_chain(STACK, PAIR, BF16),
            per_chain(2 * STACK, PAIR, BF16),
            per_chain(1, PAIR, F32),
        ],
        compiler_params=_params("parallel", "arbitrary"),
    )(r, v, kk, kd, bb, lw)


def _nat_bias_table(rpb):
    d = jnp.array([[d for d, _ in v] for v in NAT_VARIANTS])[:, :, None]
    start = jnp.array([[s for _, s in v] for v in NAT_VARIANTS])[:, :, None]
    win_row = jnp.arange(NAT_KEY_ROWS)[None, None, :] - start
    row_ok = (win_row >= 0) & (win_row < WIN_H)
    row_rel = jnp.clip(win_row - d + (WIN_H - 1), 0, 2 * WIN_H - 2)
    cols = jnp.arange(GRID_W)
    col_start = jnp.clip(cols - WIN_W // 2, 0, GRID_W - WIN_W)
    col_ok = (cols[None, :] >= col_start[:, None]) & (cols[None, :] < col_start[:, None] + WIN_W)
    col_rel = cols[None, :] - cols[:, None] + (WIN_W - 1)
    pick = (col_ok[:, :, None] & (col_rel[:, :, None] == jnp.arange(2 * WIN_W - 1))).astype(F32)
    tbl = jnp.einsum('hvqar,jcr->hvqjac', rpb.astype(F32)[:, row_rel], pick, precision=HIGHEST)
    ok = row_ok[None, :, :, None, :, None] & col_ok[None, None, None, :, None, :]
    tbl = jnp.where(ok, tbl, MASK_VALUE)
    return tbl.reshape(rpb.shape[0], len(NAT_VARIANTS), NAT_ROWS * GRID_W, NAT_KEY_ROWS * GRID_W)


def _nat_row_start(i, n_rows):
    return jnp.clip(i - WIN_H // 2, 0, n_rows - WIN_H)


def _nat_key_start(g, n_rows):
    return jnp.minimum(_nat_row_start(NAT_ROWS * g, n_rows), n_rows - NAT_KEY_ROWS)


def _nat_variant(g, n_groups):
    return jnp.where(g < 2, g, jnp.where(g >= n_groups - 2, g - (n_groups - 5), 2))


def _natten_kernel(n_rows, q_ref, k_ref, v_ref, bias_ref, o_ref):
    g = pl.program_id(1)
    n_pair = q_ref.shape[1] // PAIR
    n_q = NAT_ROWS * GRID_W
    n_keys = NAT_KEY_ROWS * GRID_W
    key_rows = pl.ds(pl.multiple_of(_nat_key_start(g, n_rows) * GRID_W, GRID_W), n_keys)
    lane_lo = lax.broadcasted_iota(jnp.int32, (n_q, PAIR), 1) < HEAD_DIM
    scale = HEAD_DIM ** -0.5
    heads = range(2 * n_pair)
    lanes = [slice((h // 2) * PAIR, (h // 2 + 1) * PAIR) for h in heads]
    s = []
    for h in heads:
        q = q_ref[:, lanes[h]] * scale
        q_h = jnp.where(lane_lo if h % 2 == 0 else ~lane_lo, q, jnp.zeros_like(q))
        s.append(_dot_nt(q_h, k_ref[key_rows, lanes[h]]) + bias_ref[h, 0])
    e = [jnp.exp(x - jnp.max(x, axis=-1, keepdims=True)) for x in s]
    denom = [jnp.sum(x, axis=-1, keepdims=True) for x in e]
    outs = [_dot(e[h].astype(BF16), v_ref[key_rows, lanes[h]]) / denom[h] for h in heads]
    for p in range(n_pair):
        o_ref[:, lanes[2 * p]] = jnp.where(lane_lo, outs[2 * p], outs[2 * p + 1]).astype(o_ref.dtype)


def _natten(qkv, bias_tbl):
    b, t, w3 = qkv.shape
    w = w3 // 3
    n_rows = t // GRID_W
    n_groups = n_rows // NAT_ROWS
    n_heads = w // HEAD_DIM
    assert n_rows % NAT_ROWS == 0 and n_groups >= 5
    n_q = NAT_ROWS * GRID_W
    bias_map = lambda bi, g: (0, _nat_variant(g, n_groups), 0, 0)
    return pl.pallas_call(
        functools.partial(_natten_kernel, n_rows),
        grid=(b, n_groups),
        in_specs=[
            pl.BlockSpec((None, n_q, w), lambda bi, g: (bi, g, 0)),
            pl.BlockSpec((None, t, w), lambda bi, g: (bi, 0, 1)),
            pl.BlockSpec((None, t, w), lambda bi, g: (bi, 0, 2)),
            pl.BlockSpec((n_heads, 1, n_q, NAT_KEY_ROWS * GRID_W), bias_map),
        ],
        out_specs=pl.BlockSpec((None, n_q, w), lambda bi, g: (bi, g, 0)),
        out_shape=jax.ShapeDtypeStruct((b, t, w), BF16),
        compiler_params=_params("parallel", "arbitrary"),
    )(qkv, qkv, qkv, bias_tbl)


def _mix_out_kernel(x_ref, yf_ref, yb_ref, r_ref, v_ref, kdf_ref, kdb_ref, g_ref, onat_ref,
                    bonus_ref, gnw_ref, gnb_ref, wa_ref, wb_ref, gpost_ref, o_ref):
    n_pair = yf_ref.shape[1] // PAIR
    head_ones = _head_ones()
    acc = _dot(onat_ref[...], wb_ref[...])
    pairs = range(n_pair)
    lanes = [slice(p * PAIR, (p + 1) * PAIR) for p in pairs]
    y = [yf_ref[:, l] + yb_ref[:, l] for l in lanes]
    mean = [_split_dot(a, head_ones, 2) * (1.0 / HEAD_DIM) for a in y]
    yc = [a - m for a, m in zip(y, mean)]
    var = [_split_dot(a * a, head_ones, 2) * (1.0 / HEAD_DIM) for a in yc]
    rk = [r_ref[:, l].astype(F32) * (0.5 * (kdf_ref[:, l].astype(F32) + kdb_ref[:, l].astype(F32)))
          * bonus_ref[:, l] for l in lanes]
    bonus = [_split_dot(a, head_ones, 2) for a in rk]
    cols = []
    for p in pairs:
        l = lanes[p]
        yn = yc[p] * lax.rsqrt(var[p] + GN_EPS) * gnw_ref[:, l] + gnb_ref[:, l]
        cols.append(((yn + bonus[p] * v_ref[:, l].astype(F32)) * g_ref[:, l].astype(F32)).astype(BF16))
    for q in range(0, n_pair, 2):
        acc = acc + _dot(jnp.concatenate(cols[q:q + 2], axis=1), wa_ref[q * PAIR:(q + 2) * PAIR, :])
    o_ref[...] = x_ref[...] + _rms(acc, gpost_ref[...])


def _mix_out(x, yf, yb, r, v, kdf, kdb, g, o_nat, bonus_scale, gn_w, gn_b, w_a, w_b, g_post):
    n, d = x.shape
    w = yf.shape[1]
    tm = _pick_tile(n, 256)
    tile = lambda cols: pl.BlockSpec((tm, cols), lambda i: (i, 0))
    row = lambda cols: pl.BlockSpec((1, cols), lambda i: (0, 0))
    resident = pl.BlockSpec((w, d), lambda i: (0, 0), pipeline_mode=pl.Buffered(1))
    return pl.pallas_call(
        _mix_out_kernel,
        grid=(n // tm,),
        in_specs=[tile(d)] + [tile(w)] * 8 + [row(w)] * 3 + [resident, resident, row(d)],
        out_specs=tile(d),
        out_shape=jax.ShapeDtypeStruct((n, d), F32),
        compiler_params=_params("parallel"),
    )(x, yf, yb, r, v, kdf, kdb, g, o_nat, bonus_scale, gn_w, gn_b, w_a, w_b, g_post)


def _encoder_layer(x, p, bias_tbl):
    b, t, d = x.shape
    n = b * t
    w = p['decay_up_fwd'].shape[1]
    x = x.reshape(n, d)
    x = _ffn(x, p['ffn1_pre_g'], p['ffn1_post_g'], p['ffn1_w_gate'], p['ffn1_w_up'], p['ffn1_w_down'])

    z_rwkv, qkv = _norm_proj(x, p['mix_pre_g'], p['w_in_rwkv'], p['w_in_nat'])

    r, v, kk, kdf, kdb, bf, bb, lwf, lwb, g = _rwkv_prep(
        z_rwkv, t, p['rwkv_shift_mix'],
        p['decay_bias_fwd'], p['decay_up_fwd'], p['decay_bias_bwd'], p['decay_up_bwd'],
        p['iclr_bias_fwd'], p['iclr_up_fwd'], p['iclr_bias_bwd'], p['iclr_up_bwd'], p['gate_up'],
        p['key_norm_scale'], p['key_iclr_mix'])
    seq = lambda a: a.reshape(b, t, w)
    yf = _wkv(seq(r), seq(v), seq(kk), seq(kdf), seq(bf), seq(lwf), False)
    yb = _wkv(seq(r), seq(v), seq(kk), seq(kdb), seq(bb), seq(lwb), True)
    o_nat = _natten(qkv.reshape(b, t, -1), bias_tbl).reshape(n, -1)

    x = _mix_out(x, yf.reshape(n, w), yb.reshape(n, w), r, v, kdf, kdb, g, o_nat,
                 p['bonus_scale'], p['gn_w'], p['gn_b'],
                 p['w_out_rwkv'], p['w_out_nat'], p['mix_post_g'])
    x = _ffn(x, p['ffn2_pre_g'], p['ffn2_post_g'], p['ffn2_w_gate'], p['ffn2_w_up'], p['ffn2_w_down'])
    return x.reshape(b, t, d)


_ROW_PARAMS = ('ffn1_pre_g', 'ffn1_post_g', 'mix_pre_g', 'rwkv_shift_mix',
               'decay_bias_fwd', 'decay_bias_bwd', 'iclr_bias_fwd', 'iclr_bias_bwd',
               'key_norm_scale', 'key_iclr_mix', 'bonus_scale', 'gn_w', 'gn_b',
               'mix_post_g', 'ffn2_pre_g', 'ffn2_post_g')
_BF16_MATS = ('ffn1_w_gate', 'ffn1_w_up', 'ffn1_w_down', 'ffn2_w_gate', 'ffn2_w_up', 'ffn2_w_down',
              'decay_up_fwd', 'decay_up_bwd', 'iclr_up_fwd', 'iclr_up_bwd', 'gate_up')


def _layer_params(weights, l):
    p = {name: weights[name][l].reshape(1, -1) for name in _ROW_PARAMS}
    p.update({name: weights[name][l].astype(BF16) for name in _BF16_MATS})
    w = weights['decay_up_fwd'].shape[-1]
    rwkv_cols = 3 * w + DECAY_RANK + ICLR_RANK + GATE_RANK
    w_in = weights['w_in'][l].astype(BF16)
    p['w_in_rwkv'] = w_in[:, :rwkv_cols]
    p['w_in_nat'] = w_in[:, rwkv_cols:]
    w_out = weights['w_out'][l].astype(BF16)
    p['w_out_rwkv'] = w_out[:w]
    p['w_out_nat'] = w_out[w:]
    return p, _nat_bias_table(weights['nat_rpb'][l])


def kernel(x_prompt, x_sample, ffn1_pre_g, ffn1_post_g, ffn1_w_gate, ffn1_w_up, ffn1_w_down, mix_pre_g, w_in, rwkv_shift_mix, decay_bias_fwd, decay_up_fwd, decay_bias_bwd, decay_up_bwd, iclr_bias_fwd, iclr_up_fwd, iclr_bias_bwd, iclr_up_bwd, gate_up, key_norm_scale, key_iclr_mix, bonus_scale, gn_w, gn_b, nat_rpb, w_out, mix_post_g, ffn2_pre_g, ffn2_post_g, ffn2_w_gate, ffn2_w_up, ffn2_w_down):
    weights = dict(
        ffn1_pre_g=ffn1_pre_g, ffn1_post_g=ffn1_post_g, ffn1_w_gate=ffn1_w_gate, ffn1_w_up=ffn1_w_up,
        ffn1_w_down=ffn1_w_down, mix_pre_g=mix_pre_g, w_in=w_in, rwkv_shift_mix=rwkv_shift_mix,
        decay_bias_fwd=decay_bias_fwd, decay_up_fwd=decay_up_fwd, decay_bias_bwd=decay_bias_bwd,
        decay_up_bwd=decay_up_bwd, iclr_bias_fwd=iclr_bias_fwd, iclr_up_fwd=iclr_up_fwd,
        iclr_bias_bwd=iclr_bias_bwd, iclr_up_bwd=iclr_up_bwd, gate_up=gate_up,
        key_norm_scale=key_norm_scale, key_iclr_mix=key_iclr_mix, bonus_scale=bonus_scale,
        gn_w=gn_w, gn_b=gn_b, nat_rpb=nat_rpb, w_out=w_out, mix_post_g=mix_post_g,
        ffn2_pre_g=ffn2_pre_g, ffn2_post_g=ffn2_post_g, ffn2_w_gate=ffn2_w_gate, ffn2_w_up=ffn2_w_up,
        ffn2_w_down=ffn2_w_down)
    depth = w_in.shape[0]
    layers = [_layer_params(weights, l) for l in range(depth)]
    outs = []
    for x in (x_prompt, x_sample):
        for p, bias_tbl in layers:
            x = _encoder_layer(x, p, bias_tbl)
        outs.append(x)
    return tuple(outs)
```

```python
import functools
import math

import jax
import jax.numpy as jnp
from jax import lax
from jax.experimental import pallas as pl
from jax.experimental.pallas import tpu as pltpu

F32 = jnp.float32
BF16 = jnp.bfloat16
HIGHEST = lax.Precision.HIGHEST

HEAD_DIM = 64
GRID_W = 64
WIN_H = 8
WIN_W = 16
DECAY_RANK = 64
ICLR_RANK = 64
GATE_RANK = 128
RMS_EPS = 1e-6
GN_EPS = 64e-5
DECAY_OFFSET = 0.5
FFN_RESIDUAL = 0.5
MASK_VALUE = -1e30

LANES = 128
HALO_ROWS = 16
EDGE_ROWS = 8
VMEM_LIMIT_BYTES = 56 * 1024 * 1024

PAIR = 2 * HEAD_DIM
CHUNK = 64
STACK = 2 * CHUNK
NAT_HEAD_GROUP = 4
NAT_ROWS = 2
NAT_KEY_ROWS = WIN_H + NAT_ROWS - 1
NAT_VARIANTS = (((0, 0), (1, 0)), ((2, 0), (3, 0)), ((4, 0), (4, 1)), ((4, 1), (5, 1)), ((6, 1), (7, 1)))
WKV_BLOCK = 512
WKV_GROUP = 2


def _params(*semantics):
    return pltpu.CompilerParams(dimension_semantics=semantics, vmem_limit_bytes=VMEM_LIMIT_BYTES)


def _dot(a, b, precision=None):
    return jnp.dot(a, b, preferred_element_type=F32, precision=precision)


def _dot_nt(a, b):
    return lax.dot_general(a, b, (((1,), (1,)), ((), ())), preferred_element_type=F32)


def _dot_tn(a, b):
    return lax.dot_general(a, b, (((0,), (0,)), ((), ())), preferred_element_type=F32)


def _rms(x, g):
    return x * lax.rsqrt(jnp.mean(x * x, axis=-1, keepdims=True) + RMS_EPS) * g


def _sigmoid(x):
    return 0.5 * jnp.tanh(0.5 * x) + 0.5


def _head_ones():
    return jnp.where(
        (lax.broadcasted_iota(jnp.int32, (PAIR, PAIR), 0) // HEAD_DIM)
        == (lax.broadcasted_iota(jnp.int32, (PAIR, PAIR), 1) // HEAD_DIM), 1.0, 0.0).astype(BF16)


def _split_dot(x, w, terms, lhs=False):
    acc = None
    rem = x
    for t in range(terms):
        piece = rem.astype(BF16)
        d = _dot(w, piece) if lhs else _dot(piece, w)
        acc = d if acc is None else acc + d
        if t + 1 < terms:
            rem = rem - piece.astype(F32)
    return acc


def _pick_tile(n, target):
    t = min(n, target)
    while n % t:
        t //= 2
    return t


def _ffn_kernel(x_ref, gpre_ref, gpost_ref, wg_ref, wu_ref, wd_ref, o_ref, h_ref, acc_ref):
    j = pl.program_id(1)

    @pl.when(j == 0)
    def _():
        h_ref[...] = _rms(x_ref[...], gpre_ref[...]).astype(BF16)
        acc_ref[...] = jnp.zeros_like(acc_ref)

    h = h_ref[...]
    g = _dot(h, wg_ref[...])
    u = _dot(h, wu_ref[...])
    act = (g * _sigmoid(g) * u).astype(BF16)
    acc_ref[...] += _dot(act, wd_ref[...])

    @pl.when(j == pl.num_programs(1) - 1)
    def _():
        o_ref[...] = x_ref[...] + FFN_RESIDUAL * _rms(acc_ref[...], gpost_ref[...])


def _ffn(x, g_pre, g_post, w_gate, w_up, w_down):
    n, d = x.shape
    f = w_gate.shape[1]
    tm = _pick_tile(n, 512)
    tf = _pick_tile(f, 512)
    return pl.pallas_call(
        _ffn_kernel,
        grid=(n // tm, f // tf),
        in_specs=[
            pl.BlockSpec((tm, d), lambda i, j: (i, 0)),
            pl.BlockSpec((1, d), lambda i, j: (0, 0)),
            pl.BlockSpec((1, d), lambda i, j: (0, 0)),
            pl.BlockSpec((d, tf), lambda i, j: (0, j)),
            pl.BlockSpec((d, tf), lambda i, j: (0, j)),
            pl.BlockSpec((tf, d), lambda i, j: (j, 0)),
        ],
        out_specs=pl.BlockSpec((tm, d), lambda i, j: (i, 0)),
        out_shape=jax.ShapeDtypeStruct((n, d), F32),
        scratch_shapes=[pltpu.VMEM((tm, d), BF16), pltpu.VMEM((tm, d), F32)],
        compiler_params=_params("parallel", "arbitrary"),
    )(x, g_pre, g_post, w_gate, w_up, w_down)


def _norm_proj_kernel(x_ref, g_ref, wa_ref, wb_ref, oa_ref, ob_ref):
    h = _rms(x_ref[...], g_ref[...]).astype(BF16)
    oa_ref[...] = _dot(h, wa_ref[...]).astype(oa_ref.dtype)
    ob_ref[...] = _dot(h, wb_ref[...]).astype(ob_ref.dtype)


def _norm_proj(x, g, w_a, w_b):
    n, d = x.shape
    ca, cb = w_a.shape[1], w_b.shape[1]
    tm = _pick_tile(n, 256)
    resident = lambda cols: pl.BlockSpec((d, cols), lambda i: (0, 0), pipeline_mode=pl.Buffered(1))
    return pl.pallas_call(
        _norm_proj_kernel,
        grid=(n // tm,),
        in_specs=[
            pl.BlockSpec((tm, d), lambda i: (i, 0)),
            pl.BlockSpec((1, d), lambda i: (0, 0)),
            resident(ca),
            resident(cb),
        ],
        out_specs=[pl.BlockSpec((tm, ca), lambda i: (i, 0)), pl.BlockSpec((tm, cb), lambda i: (i, 0))],
        out_shape=[jax.ShapeDtypeStruct((n, ca), BF16), jax.ShapeDtypeStruct((n, cb), BF16)],
        compiler_params=_params("parallel"),
    )(x, g, w_a, w_b)


def _rwkv_prep_kernel(seq_len, z_ref, zp_ref, zn_ref, mu_ref,
                      dbf_ref, duf_ref, dbb_ref, dub_ref, ibf_ref, iuf_ref, ibb_ref, iub_ref, gu_ref,
                      kns_ref, mix_ref,
                      r_ref, v_ref, kk_ref, kdf_ref, kdb_ref, bf_ref, bb_ref, lwf_ref, lwb_ref, g_ref, u_ref):
    i = pl.program_id(0)
    tm = z_ref.shape[0]
    w = r_ref.shape[1]
    ti = lax.broadcasted_iota(jnp.int32, (tm, tm), 0)
    tj = lax.broadcasted_iota(jnp.int32, (tm, tm), 1)
    neighbours = jnp.where((ti == tj + 1) | (ti + 1 == tj), 1.0, 0.0).astype(BF16)
    z = z_ref[...]
    mu = mu_ref[...]
    u_ref[...] = (1.0 - mu) * z + (0.5 * mu) * _dot(neighbours, z)
    at_seq_start = (i * tm) % seq_len == 0
    at_seq_end = ((i + 1) * tm) % seq_len == 0
    prev_row = jnp.where(at_seq_start, 0.0, zp_ref[HALO_ROWS - 1:HALO_ROWS, :].astype(F32))
    next_row = jnp.where(at_seq_end, 0.0, zn_ref[0:1, :].astype(F32))
    edge = lax.broadcasted_iota(jnp.int32, (EDGE_ROWS, 1), 0)
    u_ref[0:EDGE_ROWS, :] += jnp.where(edge == 0, 0.5 * mu * prev_row, 0.0)
    u_ref[tm - EDGE_ROWS:tm, :] += jnp.where(edge == EDGE_ROWS - 1, 0.5 * mu * next_row, 0.0)

    r_ref[...] = u_ref[:, 0:w].astype(r_ref.dtype)
    v_ref[...] = u_ref[:, 2 * w:3 * w].astype(v_ref.dtype)
    o = 3 * w
    dec_lo = jnp.tanh(u_ref[:, o:o + DECAY_RANK]).astype(BF16)
    o += DECAY_RANK
    iclr_lo = u_ref[:, o:o + ICLR_RANK].astype(BF16)
    o += ICLR_RANK
    gate_lo = _sigmoid(u_ref[:, o:o + GATE_RANK]).astype(BF16)

    scale = -math.exp(-DECAY_OFFSET)
    lwf_ref[...] = scale * _sigmoid(dbf_ref[...] + _dot(dec_lo, duf_ref[...]))
    lwb_ref[...] = scale * _sigmoid(dbb_ref[...] + _dot(dec_lo, dub_ref[...]))
    g_ref[...] = _dot(gate_lo, gu_ref[...]).astype(g_ref.dtype)

    head_ones = _head_ones()
    for p in range(w // PAIR):
        lanes = slice(p * PAIR, (p + 1) * PAIR)
        k = u_ref[:, w + p * PAIR:w + (p + 1) * PAIR]
        kn = k * kns_ref[:, lanes]
        norm = jnp.sqrt(_split_dot(kn * kn, head_ones, 2))
        kk = kn / jnp.maximum(norm, 1e-12)
        kk_ref[:, lanes] = kk.astype(kk_ref.dtype)
        mix = mix_ref[:, lanes]
        for ib_ref, iu_ref, kd_ref, b_ref in ((ibf_ref, iuf_ref, kdf_ref, bf_ref),
                                              (ibb_ref, iub_ref, kdb_ref, bb_ref)):
            al = _sigmoid(ib_ref[:, lanes] + _dot(iclr_lo, iu_ref[:, lanes]))
            kd_ref[:, lanes] = (k * (1.0 + (al - 1.0) * mix)).astype(kd_ref.dtype)
            b_ref[:, lanes] = (kk * al).astype(b_ref.dtype)


def _rwkv_prep(z, seq_len, mu, dbf, duf, dbb, dub, ibf, iuf, ibb, iub, gate_up, kns, mix):
    n, c = z.shape
    assert z.dtype == BF16
    w = duf.shape[1]
    tm = _pick_tile(seq_len, 256)
    hb = tm // HALO_ROWS
    last_hb = n // HALO_ROWS - 1
    row_spec = lambda cols: pl.BlockSpec((1, cols), lambda i: (0, 0))
    mat_spec = lambda rows: pl.BlockSpec((rows, w), lambda i: (0, 0))
    out_spec = pl.BlockSpec((tm, w), lambda i: (i, 0))
    outs = [jax.ShapeDtypeStruct((n, w), dt) for dt in (BF16,) * 7 + (F32, F32, BF16)]
    return pl.pallas_call(
        functools.partial(_rwkv_prep_kernel, seq_len),
        grid=(n // tm,),
        in_specs=[
            pl.BlockSpec((tm, c), lambda i: (i, 0)),
            pl.BlockSpec((HALO_ROWS, c), lambda i: (jnp.maximum(i * hb - 1, 0), 0)),
            pl.BlockSpec((HALO_ROWS, c), lambda i: (jnp.minimum((i + 1) * hb, last_hb), 0)),
            row_spec(c),
            row_spec(w), mat_spec(DECAY_RANK), row_spec(w), mat_spec(DECAY_RANK),
            row_spec(w), mat_spec(ICLR_RANK), row_spec(w), mat_spec(ICLR_RANK),
            mat_spec(GATE_RANK), row_spec(w), row_spec(w),
        ],
        out_specs=[out_spec] * 10,
        out_shape=outs,
        scratch_shapes=[pltpu.VMEM((tm, c), F32)],
        compiler_params=_params("parallel"),
    )(z, z, z, mu, dbf, duf, dbb, dub, ibf, iuf, ibb, iub, gate_up, kns, mix)


def _stack(x, lane_lo):
    return jnp.concatenate([jnp.where(lane_lo, x, 0.0), jnp.where(lane_lo, 0.0, x)], axis=0)


def _wkv_kernel(reverse, r_ref, v_ref, kk_ref, kd_ref, b_ref, lw_ref, y_ref,
                s_ref, wr_s, u0_s, ar_s, vs_s, bk_s, dec_s):
    tc = pl.program_id(1)
    n_sub = r_ref.shape[0] // CHUNK
    n_pair = r_ref.shape[1] // PAIR

    @pl.when(tc == 0)
    def _():
        s_ref[...] = jnp.zeros_like(s_ref)

    ii = lax.broadcasted_iota(jnp.int32, (STACK, STACK), 0)
    jj = lax.broadcasted_iota(jnp.int32, (STACK, STACK), 1)
    ti = ii % CHUNK
    tj = jj % CHUNK
    same_head = (ii // CHUNK) == (jj // CHUNK)
    ii2 = lax.broadcasted_iota(jnp.int32, (STACK, 2 * STACK), 0)
    jj2 = lax.broadcasted_iota(jnp.int32, (STACK, 2 * STACK), 1)
    same_head2 = (ii2 // CHUNK) == ((jj2 % STACK) // CHUNK)
    if reverse:
        strict = same_head & (ti < tj)
        incl2 = same_head2 & (ii2 % CHUNK <= jj2 % CHUNK)
    else:
        strict = same_head & (ti > tj)
        incl2 = same_head2 & (ii2 % CHUNK >= jj2 % CHUNK)
    ci = lax.broadcasted_iota(jnp.int32, (CHUNK, CHUNK), 0)
    cj = lax.broadcasted_iota(jnp.int32, (CHUNK, CHUNK), 1)
    cum_tri = jnp.where((ci <= cj) if reverse else (ci >= cj), 1.0, 0.0).astype(BF16)
    eye = jnp.where(ii == jj, 1.0, 0.0).astype(F32)
    lane_lo = lax.broadcasted_iota(jnp.int32, (CHUNK, PAIR), 1) < HEAD_DIM
    end_row = 0 if reverse else CHUNK - 1

    level_masks = []
    m = 1
    while m < CHUNK:
        level_masks.append(strict & (ti // (2 * m) == tj // (2 * m)) & (ti // m != tj // m))
        m *= 2

    pairs = range(n_pair)
    lanes = [slice(p * PAIR, (p + 1) * PAIR) for p in pairs]

    def chunk_operands(chunk_ids):
        chunk_rows = [pl.ds(c * CHUNK, CHUNK) for c in chunk_ids]
        chunk_cum = [_split_dot(lw_ref[rows, :], cum_tri, 3, lhs=True) for rows in chunk_rows]
        chains = [(chunk_ids[q], p) for q in range(WKV_GROUP) for p in pairs]
        ops = []
        for i, (c, p) in enumerate(chains):
            q = i // n_pair
            rows = chunk_rows[q]
            r = r_ref[rows, lanes[p]].astype(F32)
            v = v_ref[rows, lanes[p]].astype(F32)
            kk = kk_ref[rows, lanes[p]].astype(F32)
            kd = kd_ref[rows, lanes[p]].astype(F32)
            b = b_ref[rows, lanes[p]].astype(F32)
            lw = lw_ref[rows, lanes[p]]

            cum = chunk_cum[q][:, lanes[p]]
            tot = cum[end_row:end_row + 1, :]
            e_in = jnp.exp(cum)
            e_ex = jnp.exp(cum - lw)
            e_inv = jnp.exp(-cum)
            e_end = jnp.exp(tot - cum)
            rt = _stack(r * e_in, lane_lo).astype(BF16)
            vs = _stack(v, lane_lo).astype(BF16)
            vs_s[c, p] = vs
            bk_s[c, p] = jnp.concatenate([_stack(b * e_end, lane_lo), _stack(kd * e_end, lane_lo)],
                                         axis=0).astype(BF16)
            dec_s[c, p] = jnp.exp(tot)
            ops.append(dict(
                rt=rt, vs=vs,
                at=_stack(-kk * e_ex, lane_lo).astype(BF16),
                bt=_stack(b * e_inv, lane_lo).astype(BF16),
                kt=_stack(kd * e_inv, lane_lo).astype(BF16)))

        m1 = [_dot_nt(jnp.concatenate([o['at'], o['rt']], axis=0),
                      jnp.concatenate([o['bt'], o['kt']], axis=0)) for o in ops]
        l_ab = [m[:STACK, :STACK] for m in m1]
        l_ak = [jnp.where(strict, m[:STACK, STACK:], 0.0).astype(BF16) for m in m1]
        for (c, p), m in zip(chains, m1):
            ar_s[c, p] = jnp.where(incl2, m[STACK:, :], 0.0).astype(BF16)

        inv = [eye + jnp.where(level_masks[0], l, 0.0) for l in l_ab]
        for mask in level_masks[1:]:
            inv_b = [x.astype(BF16) for x in inv]
            t = [_dot(jnp.where(mask, l, 0.0).astype(BF16), x).astype(BF16) for l, x in zip(l_ab, inv_b)]
            inv = [x + _dot(xb, y) for x, xb, y in zip(inv, inv_b, t)]
        inv_b = [x.astype(BF16) for x in inv]

        lakv = [_dot(l, o['vs']).astype(BF16) for l, o in zip(l_ak, ops)]
        for (c, p), x, o, y in zip(chains, inv_b, ops, lakv):
            wu = _dot(x, jnp.concatenate([o['at'], y], axis=1))
            wr_s[c, p] = jnp.concatenate([wu[:, :PAIR].astype(BF16), o['rt']], axis=0)
            u0_s[c, p] = wu[:, PAIR:]

    def advance(c):
        rows = pl.ds(c * CHUNK, CHUNK)
        s = [s_ref[p] for p in pairs]
        ur = [_dot_nt(wr_s[c, p], s[p].astype(BF16)) for p in pairs]
        uv = [jnp.concatenate([(ur[p][:STACK] + u0_s[c, p]).astype(BF16), vs_s[c, p]], axis=0)
              for p in pairs]
        for p in pairs:
            s_ref[p] = s[p] * dec_s[c, p] + _dot_tn(uv[p], bk_s[c, p])
        for p in pairs:
            ys = ur[p][STACK:] + _dot(ar_s[c, p], uv[p])
            y_ref[rows, lanes[p]] = ys[:CHUNK] + ys[CHUNK:]

    order = list(range(n_sub))[::-1] if reverse else list(range(n_sub))
    groups = [order[i:i + WKV_GROUP] for i in range(0, n_sub, WKV_GROUP)]
    chunk_operands(groups[0])
    for gi, group in enumerate(groups):
        if gi + 1 < len(groups):
            chunk_operands(groups[gi + 1])
        for c in group:
            advance(c)


def _wkv(r, v, kk, kd, bb, lw, reverse):
    b, t, w = r.shape
    lc = _pick_tile(t, WKV_BLOCK)
    n_t = t // lc
    n_sub = lc // CHUNK
    n_pair = w // PAIR
    tmap = (lambda bi, ti: (bi, n_t - 1 - ti, 0)) if reverse else (lambda bi, ti: (bi, ti, 0))
    seq_spec = pl.BlockSpec((None, lc, w), tmap)
    per_chain = lambda rows, cols, dtype: pltpu.VMEM((n_sub, n_pair, rows, cols), dtype)
    return pl.pallas_call(
        functools.partial(_wkv_kernel, reverse),
        grid=(b, n_t),
        in_specs=[seq_spec] * 6,
        out_specs=seq_spec,
        out_shape=jax.ShapeDtypeStruct((b, t, w), F32),
        scratch_shapes=[
            pltpu.VMEM((n_pair, PAIR, PAIR), F32),
            per_chain(2 * STACK, PAIR, BF16),
            per_chain(STACK, PAIR, F32),
            per_chain(STACK, 2 * STACK, BF16),
            per_chain(STACK, PAIR, BF16),
            per_chain(2 * STACK, PAIR, BF16),
            per_chain(1, PAIR, F32),
        ],
        compiler_params=_params("parallel", "arbitrary"),
    )(r, v, kk, kd, bb, lw)


def _nat_bias_table(rpb):
    d = jnp.array([[d for d, _ in v] for v in NAT_VARIANTS])[:, :, None]
    start = jnp.array([[s for _, s in v] for v in NAT_VARIANTS])[:, :, None]
    win_row = jnp.arange(NAT_KEY_ROWS)[None, None, :] - start
    row_ok = (win_row >= 0) & (win_row < WIN_H)
    row_rel = jnp.clip(win_row - d + (WIN_H - 1), 0, 2 * WIN_H - 2)
    cols = jnp.arange(GRID_W)
    col_start = jnp.clip(cols - WIN_W // 2, 0, GRID_W - WIN_W)
    col_ok = (cols[None, :] >= col_start[:, None]) & (cols[None, :] < col_start[:, None] + WIN_W)
    col_rel = cols[None, :] - cols[:, None] + (WIN_W - 1)
    pick = (col_ok[:, :, None] & (col_rel[:, :, None] == jnp.arange(2 * WIN_W - 1))).astype(F32)
    tbl = jnp.einsum('hvqar,jcr->hvqjac', rpb.astype(F32)[:, row_rel], pick, precision=HIGHEST)
    ok = row_ok[None, :, :, None, :, None] & col_ok[None, None, None, :, None, :]
    tbl = jnp.where(ok, tbl, MASK_VALUE)
    return tbl.reshape(rpb.shape[0], len(NAT_VARIANTS), NAT_ROWS * GRID_W, NAT_KEY_ROWS * GRID_W)


def _nat_row_start(i, n_rows):
    return jnp.clip(i - WIN_H // 2, 0, n_rows - WIN_H)


def _nat_key_start(g, n_rows):
    return jnp.minimum(_nat_row_start(NAT_ROWS * g, n_rows), n_rows - NAT_KEY_ROWS)


def _nat_variant(g, n_groups):
    return jnp.where(g < 2, g, jnp.where(g >= n_groups - 2, g - (n_groups - 5), 2))


def _natten_kernel(n_rows, q_ref, k_ref, v_ref, bias_ref, o_ref):
    g = pl.program_id(1)
    n_pair = q_ref.shape[1] // PAIR
    n_q = NAT_ROWS * GRID_W
    n_keys = NAT_KEY_ROWS * GRID_W
    key_rows = pl.ds(pl.multiple_of(_nat_key_start(g, n_rows) * GRID_W, GRID_W), n_keys)
    lane_lo = lax.broadcasted_iota(jnp.int32, (n_q, PAIR), 1) < HEAD_DIM
    scale = HEAD_DIM ** -0.5
    for h0 in range(0, 2 * n_pair, NAT_HEAD_GROUP):
        heads = range(h0, h0 + NAT_HEAD_GROUP)
        lanes = {h: slice((h // 2) * PAIR, (h // 2 + 1) * PAIR) for h in heads}
        s = {}
        for h in heads:
            q = q_ref[:, lanes[h]] * scale
            q_h = jnp.where(lane_lo if h % 2 == 0 else ~lane_lo, q, jnp.zeros_like(q))
            s[h] = _dot_nt(q_h, k_ref[key_rows, lanes[h]]) + bias_ref[h, 0]
        e = {h: jnp.exp(s[h] - jnp.max(s[h], axis=-1, keepdims=True)) for h in heads}
        denom = {h: jnp.sum(e[h], axis=-1, keepdims=True) for h in heads}
        outs = {h: _dot(e[h].astype(BF16), v_ref[key_rows, lanes[h]]) / denom[h] for h in heads}
        for h in range(h0, h0 + NAT_HEAD_GROUP, 2):
            o_ref[:, lanes[h]] = jnp.where(lane_lo, outs[h], outs[h + 1]).astype(o_ref.dtype)


def _natten(qkv, bias_tbl):
    b, t, w3 = qkv.shape
    w = w3 // 3
    n_rows = t // GRID_W
    n_groups = n_rows // NAT_ROWS
    n_heads = w // HEAD_DIM
    assert n_rows % NAT_ROWS == 0 and n_groups >= 5
    n_q = NAT_ROWS * GRID_W
    bias_map = lambda bi, g: (0, _nat_variant(g, n_groups), 0, 0)
    return pl.pallas_call(
        functools.partial(_natten_kernel, n_rows),
        grid=(b, n_groups),
        in_specs=[
            pl.BlockSpec((None, n_q, w), lambda bi, g: (bi, g, 0)),
            pl.BlockSpec((None, t, w), lambda bi, g: (bi, 0, 1)),
            pl.BlockSpec((None, t, w), lambda bi, g: (bi, 0, 2)),
            pl.BlockSpec((n_heads, 1, n_q, NAT_KEY_ROWS * GRID_W), bias_map),
        ],
        out_specs=pl.BlockSpec((None, n_q, w), lambda bi, g: (bi, g, 0)),
        out_shape=jax.ShapeDtypeStruct((b, t, w), BF16),
        compiler_params=_params("parallel", "arbitrary"),
    )(qkv, qkv, qkv, bias_tbl)


def _mix_out_kernel(x_ref, yf_ref, yb_ref, r_ref, v_ref, kdf_ref, kdb_ref, g_ref, onat_ref,
                    bonus_ref, gnw_ref, gnb_ref, wa_ref, wb_ref, gpost_ref, o_ref):
    n_pair = yf_ref.shape[1] // PAIR
    head_ones = _head_ones()
    acc = _dot(onat_ref[...], wb_ref[...])
    pairs = range(n_pair)
    lanes = [slice(p * PAIR, (p + 1) * PAIR) for p in pairs]
    y = [yf_ref[:, l] + yb_ref[:, l] for l in lanes]
    mean = [_split_dot(a, head_ones, 2) * (1.0 / HEAD_DIM) for a in y]
    yc = [a - m for a, m in zip(y, mean)]
    var = [_split_dot(a * a, head_ones, 2) * (1.0 / HEAD_DIM) for a in yc]
    rk = [r_ref[:, l].astype(F32) * (0.5 * (kdf_ref[:, l].astype(F32) + kdb_ref[:, l].astype(F32)))
          * bonus_ref[:, l] for l in lanes]
    bonus = [_split_dot(a, head_ones, 2) for a in rk]
    cols = []
    for p in pairs:
        l = lanes[p]
        yn = yc[p] * lax.rsqrt(var[p] + GN_EPS) * gnw_ref[:, l] + gnb_ref[:, l]
        cols.append(((yn + bonus[p] * v_ref[:, l].astype(F32)) * g_ref[:, l].astype(F32)).astype(BF16))
    for q in range(0, n_pair, 2):
        acc = acc + _dot(jnp.concatenate(cols[q:q + 2], axis=1), wa_ref[q * PAIR:(q + 2) * PAIR, :])
    o_ref[...] = x_ref[...] + _rms(acc, gpost_ref[...])


def _mix_out(x, yf, yb, r, v, kdf, kdb, g, o_nat, bonus_scale, gn_w, gn_b, w_a, w_b, g_post):
    n, d = x.shape
    w = yf.shape[1]
    tm = _pick_tile(n, 256)
    tile = lambda cols: pl.BlockSpec((tm, cols), lambda i: (i, 0))
    row = lambda cols: pl.BlockSpec((1, cols), lambda i: (0, 0))
    resident = pl.BlockSpec((w, d), lambda i: (0, 0), pipeline_mode=pl.Buffered(1))
    return pl.pallas_call(
        _mix_out_kernel,
        grid=(n // tm,),
        in_specs=[tile(d)] + [tile(w)] * 8 + [row(w)] * 3 + [resident, resident, row(d)],
        out_specs=tile(d),
        out_shape=jax.ShapeDtypeStruct((n, d), F32),
        compiler_params=_params("parallel"),
    )(x, yf, yb, r, v, kdf, kdb, g, o_nat, bonus_scale, gn_w, gn_b, w_a, w_b, g_post)


def _encoder_layer(x, p, bias_tbl):
    b, t, d = x.shape
    n = b * t
    w = p['decay_up_fwd'].shape[1]
    x = x.reshape(n, d)
    x = _ffn(x, p['ffn1_pre_g'], p['ffn1_post_g'], p['ffn1_w_gate'], p['ffn1_w_up'], p['ffn1_w_down'])

    z_rwkv, qkv = _norm_proj(x, p['mix_pre_g'], p['w_in_rwkv'], p['w_in_nat'])

    r, v, kk, kdf, kdb, bf, bb, lwf, lwb, g = _rwkv_prep(
        z_rwkv, t, p['rwkv_shift_mix'],
        p['decay_bias_fwd'], p['decay_up_fwd'], p['decay_bias_bwd'], p['decay_up_bwd'],
        p['iclr_bias_fwd'], p['iclr_up_fwd'], p['iclr_bias_bwd'], p['iclr_up_bwd'], p['gate_up'],
        p['key_norm_scale'], p['key_iclr_mix'])
    seq = lambda a: a.reshape(b, t, w)
    yf = _wkv(seq(r), seq(v), seq(kk), seq(kdf), seq(bf), seq(lwf), False)
    yb = _wkv(seq(r), seq(v), seq(kk), seq(kdb), seq(bb), seq(lwb), True)
    o_nat = _natten(qkv.reshape(b, t, -1), bias_tbl).reshape(n, -1)

    x = _mix_out(x, yf.reshape(n, w), yb.reshape(n, w), r, v, kdf, kdb, g, o_nat,
                 p['bonus_scale'], p['gn_w'], p['gn_b'],
                 p['w_out_rwkv'], p['w_out_nat'], p['mix_post_g'])
    x = _ffn(x, p['ffn2_pre_g'], p['ffn2_post_g'], p['ffn2_w_gate'], p['ffn2_w_up'], p['ffn2_w_down'])
    return x.reshape(b, t, d)


_ROW_PARAMS = ('ffn1_pre_g', 'ffn1_post_g', 'mix_pre_g', 'rwkv_shift_mix',
               'decay_bias_fwd', 'decay_bias_bwd', 'iclr_bias_fwd', 'iclr_bias_bwd',
               'key_norm_scale', 'key_iclr_mix', 'bonus_scale', 'gn_w', 'gn_b',
               'mix_post_g', 'ffn2_pre_g', 'ffn2_post_g')
_BF16_MATS = ('ffn1_w_gate', 'ffn1_w_up', 'ffn1_w_down', 'ffn2_w_gate', 'ffn2_w_up', 'ffn2_w_down',
              'decay_up_fwd', 'decay_up_bwd', 'iclr_up_fwd', 'iclr_up_bwd', 'gate_up')


def _layer_params(weights, l):
    p = {name: weights[name][l].reshape(1, -1) for name in _ROW_PARAMS}
    p.update({name: weights[name][l].astype(BF16) for name in _BF16_MATS})
    w = weights['decay_up_fwd'].shape[-1]
    rwkv_cols = 3 * w + DECAY_RANK + ICLR_RANK + GATE_RANK
    w_in = weights['w_in'][l].astype(BF16)
    p['w_in_rwkv'] = w_in[:, :rwkv_cols]
    p['w_in_nat'] = w_in[:, rwkv_cols:]
    w_out = weights['w_out'][l].astype(BF16)
    p['w_out_rwkv'] = w_out[:w]
    p['w_out_nat'] = w_out[w:]
    return p, _nat_bias_table(weights['nat_rpb'][l])


def kernel(x_prompt, x_sample, ffn1_pre_g, ffn1_post_g, ffn1_w_gate, ffn1_w_up, ffn1_w_down, mix_pre_g, w_in, rwkv_shift_mix, decay_bias_fwd, decay_up_fwd, decay_bias_bwd, decay_up_bwd, iclr_bias_fwd, iclr_up_fwd, iclr_bias_bwd, iclr_up_bwd, gate_up, key_norm_scale, key_iclr_mix, bonus_scale, gn_w, gn_b, nat_rpb, w_out, mix_post_g, ffn2_pre_g, ffn2_post_g, ffn2_w_gate, ffn2_w_up, ffn2_w_down):
    weights = dict(
        ffn1_pre_g=ffn1_pre_g, ffn1_post_g=ffn1_post_g, ffn1_w_gate=ffn1_w_gate, ffn1_w_up=ffn1_w_up,
        ffn1_w_down=ffn1_w_down, mix_pre_g=mix_pre_g, w_in=w_in, rwkv_shift_mix=rwkv_shift_mix,
        decay_bias_fwd=decay_bias_fwd, decay_up_fwd=decay_up_fwd, decay_bias_bwd=decay_bias_bwd,
        decay_up_bwd=decay_up_bwd, iclr_bias_fwd=iclr_bias_fwd, iclr_up_fwd=iclr_up_fwd,
        iclr_bias_bwd=iclr_bias_bwd, iclr_up_bwd=iclr_up_bwd, gate_up=gate_up,
        key_norm_scale=key_norm_scale, key_iclr_mix=key_iclr_mix, bonus_scale=bonus_scale,
        gn_w=gn_w, gn_b=gn_b, nat_rpb=nat_rpb, w_out=w_out, mix_post_g=mix_post_g,
        ffn2_pre_g=ffn2_pre_g, ffn2_post_g=ffn2_post_g, ffn2_w_gate=ffn2_w_gate, ffn2_w_up=ffn2_w_up,
        ffn2_w_down=ffn2_w_down)
    depth = w_in.shape[0]
    layers = [_layer_params(weights, l) for l in range(depth)]
    outs = []
    for x in (x_prompt, x_sample):
        for p, bias_tbl in layers:
            x = _encoder_layer(x, p, bias_tbl)
        outs.append(x)
    return tuple(outs)
```

```python
import functools
import math

import jax
import jax.numpy as jnp
from jax import lax
from jax.experimental import pallas as pl
from jax.experimental.pallas import tpu as pltpu

F32 = jnp.float32
BF16 = jnp.bfloat16
HIGHEST = lax.Precision.HIGHEST

HEAD_DIM = 64
GRID_W = 64
WIN_H = 8
WIN_W = 16
DECAY_RANK = 64
ICLR_RANK = 64
GATE_RANK = 128
RMS_EPS = 1e-6
GN_EPS = 64e-5
DECAY_OFFSET = 0.5
FFN_RESIDUAL = 0.5
MASK_VALUE = -1e30
LOG2_E = math.log2(math.e)

LANES = 128
HALO_ROWS = 16
EDGE_ROWS = 8
VMEM_LIMIT_BYTES = 56 * 1024 * 1024

PAIR = 2 * HEAD_DIM
CHUNK = 64
STACK = 2 * CHUNK
NAT_HEAD_GROUP = 4
NAT_ROWS = 2
NAT_KEY_ROWS = WIN_H + NAT_ROWS - 1
NAT_VARIANTS = (((0, 0), (1, 0)), ((2, 0), (3, 0)), ((4, 0), (4, 1)), ((4, 1), (5, 1)), ((6, 1), (7, 1)))
WKV_BLOCK = 512
WKV_GROUP = 2


def _params(*semantics):
    return pltpu.CompilerParams(dimension_semantics=semantics, vmem_limit_bytes=VMEM_LIMIT_BYTES)


def _dot(a, b, precision=None):
    return jnp.dot(a, b, preferred_element_type=F32, precision=precision)


def _dot_nt(a, b):
    return lax.dot_general(a, b, (((1,), (1,)), ((), ())), preferred_element_type=F32)


def _dot_tn(a, b):
    return lax.dot_general(a, b, (((0,), (0,)), ((), ())), preferred_element_type=F32)


def _rms(x, g):
    return x * lax.rsqrt(jnp.mean(x * x, axis=-1, keepdims=True) + RMS_EPS) * g


def _sigmoid(x):
    return 0.5 * jnp.tanh(0.5 * x) + 0.5


def _head_ones():
    return jnp.where(
        (lax.broadcasted_iota(jnp.int32, (PAIR, PAIR), 0) // HEAD_DIM)
        == (lax.broadcasted_iota(jnp.int32, (PAIR, PAIR), 1) // HEAD_DIM), 1.0, 0.0).astype(BF16)


def _split_dot(x, w, terms, lhs=False):
    acc = None
    rem = x
    for t in range(terms):
        piece = rem.astype(BF16)
        d = _dot(w, piece) if lhs else _dot(piece, w)
        acc = d if acc is None else acc + d
        if t + 1 < terms:
            rem = rem - piece.astype(F32)
    return acc


def _pick_tile(n, target):
    t = min(n, target)
    while n % t:
        t //= 2
    return t


def _ffn_kernel(x_ref, gpre_ref, gpost_ref, wg_ref, wu_ref, wd_ref, o_ref, h_ref, acc_ref):
    j = pl.program_id(1)

    @pl.when(j == 0)
    def _():
        h_ref[...] = _rms(x_ref[...], gpre_ref[...]).astype(BF16)
        acc_ref[...] = jnp.zeros_like(acc_ref)

    h = h_ref[...]
    g = _dot(h, wg_ref[...])
    u = _dot(h, wu_ref[...])
    act = (g * _sigmoid(g) * u).astype(BF16)
    acc_ref[...] += _dot(act, wd_ref[...])

    @pl.when(j == pl.num_programs(1) - 1)
    def _():
        o_ref[...] = x_ref[...] + FFN_RESIDUAL * _rms(acc_ref[...], gpost_ref[...])


def _ffn(x, g_pre, g_post, w_gate, w_up, w_down):
    n, d = x.shape
    f = w_gate.shape[1]
    tm = _pick_tile(n, 512)
    tf = _pick_tile(f, 512)
    return pl.pallas_call(
        _ffn_kernel,
        grid=(n // tm, f // tf),
        in_specs=[
            pl.BlockSpec((tm, d), lambda i, j: (i, 0)),
            pl.BlockSpec((1, d), lambda i, j: (0, 0)),
            pl.BlockSpec((1, d), lambda i, j: (0, 0)),
            pl.BlockSpec((d, tf), lambda i, j: (0, j)),
            pl.BlockSpec((d, tf), lambda i, j: (0, j)),
            pl.BlockSpec((tf, d), lambda i, j: (j, 0)),
        ],
        out_specs=pl.BlockSpec((tm, d), lambda i, j: (i, 0)),
        out_shape=jax.ShapeDtypeStruct((n, d), F32),
        scratch_shapes=[pltpu.VMEM((tm, d), BF16), pltpu.VMEM((tm, d), F32)],
        compiler_params=_params("parallel", "arbitrary"),
    )(x, g_pre, g_post, w_gate, w_up, w_down)


def _norm_proj_kernel(x_ref, g_ref, wa_ref, wb_ref, oa_ref, ob_ref):
    h = _rms(x_ref[...], g_ref[...]).astype(BF16)
    oa_ref[...] = _dot(h, wa_ref[...]).astype(oa_ref.dtype)
    ob_ref[...] = _dot(h, wb_ref[...]).astype(ob_ref.dtype)


def _norm_proj(x, g, w_a, w_b):
    n, d = x.shape
    ca, cb = w_a.shape[1], w_b.shape[1]
    tm = _pick_tile(n, 256)
    resident = lambda cols: pl.BlockSpec((d, cols), lambda i: (0, 0), pipeline_mode=pl.Buffered(1))
    return pl.pallas_call(
        _norm_proj_kernel,
        grid=(n // tm,),
        in_specs=[
            pl.BlockSpec((tm, d), lambda i: (i, 0)),
            pl.BlockSpec((1, d), lambda i: (0, 0)),
            resident(ca),
            resident(cb),
        ],
        out_specs=[pl.BlockSpec((tm, ca), lambda i: (i, 0)), pl.BlockSpec((tm, cb), lambda i: (i, 0))],
        out_shape=[jax.ShapeDtypeStruct((n, ca), BF16), jax.ShapeDtypeStruct((n, cb), BF16)],
        compiler_params=_params("parallel"),
    )(x, g, w_a, w_b)


def _rwkv_prep_kernel(seq_len, z_ref, zp_ref, zn_ref, mu_ref,
                      dbf_ref, duf_ref, dbb_ref, dub_ref, ibf_ref, iuf_ref, ibb_ref, iub_ref, gu_ref,
                      kns_ref, mix_ref,
                      r_ref, v_ref, kk_ref, kdf_ref, kdb_ref, bf_ref, bb_ref, lwf_ref, lwb_ref, g_ref, u_ref):
    i = pl.program_id(0)
    tm = z_ref.shape[0]
    w = r_ref.shape[1]
    ti = lax.broadcasted_iota(jnp.int32, (tm, tm), 0)
    tj = lax.broadcasted_iota(jnp.int32, (tm, tm), 1)
    neighbours = jnp.where((ti == tj + 1) | (ti + 1 == tj), 1.0, 0.0).astype(BF16)
    z = z_ref[...]
    mu = mu_ref[...]
    u_ref[...] = (1.0 - mu) * z + (0.5 * mu) * _dot(neighbours, z)
    at_seq_start = (i * tm) % seq_len == 0
    at_seq_end = ((i + 1) * tm) % seq_len == 0
    prev_row = jnp.where(at_seq_start, 0.0, zp_ref[HALO_ROWS - 1:HALO_ROWS, :].astype(F32))
    next_row = jnp.where(at_seq_end, 0.0, zn_ref[0:1, :].astype(F32))
    edge = lax.broadcasted_iota(jnp.int32, (EDGE_ROWS, 1), 0)
    u_ref[0:EDGE_ROWS, :] += jnp.where(edge == 0, 0.5 * mu * prev_row, 0.0)
    u_ref[tm - EDGE_ROWS:tm, :] += jnp.where(edge == EDGE_ROWS - 1, 0.5 * mu * next_row, 0.0)

    r_ref[...] = u_ref[:, 0:w].astype(r_ref.dtype)
    v_ref[...] = u_ref[:, 2 * w:3 * w].astype(v_ref.dtype)
    o = 3 * w
    dec_lo = jnp.tanh(u_ref[:, o:o + DECAY_RANK]).astype(BF16)
    o += DECAY_RANK
    iclr_lo = u_ref[:, o:o + ICLR_RANK].astype(BF16)
    o += ICLR_RANK
    gate_lo = _sigmoid(u_ref[:, o:o + GATE_RANK]).astype(BF16)

    scale = -math.exp(-DECAY_OFFSET)
    lwf_ref[...] = scale * _sigmoid(dbf_ref[...] + _dot(dec_lo, duf_ref[...]))
    lwb_ref[...] = scale * _sigmoid(dbb_ref[...] + _dot(dec_lo, dub_ref[...]))
    g_ref[...] = _dot(gate_lo, gu_ref[...]).astype(g_ref.dtype)

    head_ones = _head_ones()
    for p in range(w // PAIR):
        lanes = slice(p * PAIR, (p + 1) * PAIR)
        k = u_ref[:, w + p * PAIR:w + (p + 1) * PAIR]
        kn = k * kns_ref[:, lanes]
        kk = kn * lax.rsqrt(jnp.maximum(_split_dot(kn * kn, head_ones, 2), 1e-24))
        kk_ref[:, lanes] = kk.astype(kk_ref.dtype)
        mix = mix_ref[:, lanes]
        for ib_ref, iu_ref, kd_ref, b_ref in ((ibf_ref, iuf_ref, kdf_ref, bf_ref),
                                              (ibb_ref, iub_ref, kdb_ref, bb_ref)):
            al = _sigmoid(ib_ref[:, lanes] + _dot(iclr_lo, iu_ref[:, lanes]))
            kd_ref[:, lanes] = (k * (1.0 + (al - 1.0) * mix)).astype(kd_ref.dtype)
            b_ref[:, lanes] = (kk * al).astype(b_ref.dtype)


def _rwkv_prep(z, seq_len, mu, dbf, duf, dbb, dub, ibf, iuf, ibb, iub, gate_up, kns, mix):
    n, c = z.shape
    assert z.dtype == BF16
    w = duf.shape[1]
    tm = _pick_tile(seq_len, 256)
    hb = tm // HALO_ROWS
    last_hb = n // HALO_ROWS - 1
    row_spec = lambda cols: pl.BlockSpec((1, cols), lambda i: (0, 0))
    mat_spec = lambda rows: pl.BlockSpec((rows, w), lambda i: (0, 0))
    out_spec = pl.BlockSpec((tm, w), lambda i: (i, 0))
    outs = [jax.ShapeDtypeStruct((n, w), dt) for dt in (BF16,) * 7 + (F32, F32, BF16)]
    return pl.pallas_call(
        functools.partial(_rwkv_prep_kernel, seq_len),
        grid=(n // tm,),
        in_specs=[
            pl.BlockSpec((tm, c), lambda i: (i, 0)),
            pl.BlockSpec((HALO_ROWS, c), lambda i: (jnp.maximum(i * hb - 1, 0), 0)),
            pl.BlockSpec((HALO_ROWS, c), lambda i: (jnp.minimum((i + 1) * hb, last_hb), 0)),
            row_spec(c),
            row_spec(w), mat_spec(DECAY_RANK), row_spec(w), mat_spec(DECAY_RANK),
            row_spec(w), mat_spec(ICLR_RANK), row_spec(w), mat_spec(ICLR_RANK),
            mat_spec(GATE_RANK), row_spec(w), row_spec(w),
        ],
        out_specs=[out_spec] * 10,
        out_shape=outs,
        scratch_shapes=[pltpu.VMEM((tm, c), F32)],
        compiler_params=_params("parallel"),
    )(z, z, z, mu, dbf, duf, dbb, dub, ibf, iuf, ibb, iub, gate_up, kns, mix)


def _stack(x, lane_lo):
    return jnp.concatenate([jnp.where(lane_lo, x, 0.0), jnp.where(lane_lo, 0.0, x)], axis=0)


def _wkv_kernel(reverse, r_ref, v_ref, kk_ref, kd_ref, b_ref, lw_ref, y_ref,
                s_ref, wr_s, u0_s, ar_s, vs_s, bk_s, dec_s):
    tc = pl.program_id(1)
    n_sub = r_ref.shape[0] // CHUNK
    n_pair = r_ref.shape[1] // PAIR

    @pl.when(tc == 0)
    def _():
        s_ref[...] = jnp.zeros_like(s_ref)

    ii = lax.broadcasted_iota(jnp.int32, (STACK, STACK), 0)
    jj = lax.broadcasted_iota(jnp.int32, (STACK, STACK), 1)
    ti = ii % CHUNK
    tj = jj % CHUNK
    same_head = (ii // CHUNK) == (jj // CHUNK)
    ii2 = lax.broadcasted_iota(jnp.int32, (STACK, 2 * STACK), 0)
    jj2 = lax.broadcasted_iota(jnp.int32, (STACK, 2 * STACK), 1)
    same_head2 = (ii2 // CHUNK) == ((jj2 % STACK) // CHUNK)
    if reverse:
        strict = same_head & (ti < tj)
        incl2 = same_head2 & (ii2 % CHUNK <= jj2 % CHUNK)
    else:
        strict = same_head & (ti > tj)
        incl2 = same_head2 & (ii2 % CHUNK >= jj2 % CHUNK)
    ci = lax.broadcasted_iota(jnp.int32, (CHUNK, CHUNK), 0)
    cj = lax.broadcasted_iota(jnp.int32, (CHUNK, CHUNK), 1)
    cum_tri = jnp.where((ci <= cj) if reverse else (ci >= cj), 1.0, 0.0).astype(BF16)
    eye = jnp.where(ii == jj, 1.0, 0.0).astype(F32)
    lane_lo = lax.broadcasted_iota(jnp.int32, (CHUNK, PAIR), 1) < HEAD_DIM
    end_row = 0 if reverse else CHUNK - 1

    level_masks = []
    m = 1
    while m < CHUNK:
        level_masks.append(strict & (ti // (2 * m) == tj // (2 * m)) & (ti // m != tj // m))
        m *= 2

    pairs = range(n_pair)
    lanes = [slice(p * PAIR, (p + 1) * PAIR) for p in pairs]

    def chunk_operands(chunk_ids):
        chunk_rows = [pl.ds(c * CHUNK, CHUNK) for c in chunk_ids]
        chunk_cum = [_split_dot(lw_ref[rows, :], cum_tri, 3, lhs=True) for rows in chunk_rows]
        chains = [(chunk_ids[q], p) for q in range(WKV_GROUP) for p in pairs]
        ops = []
        for i, (c, p) in enumerate(chains):
            q = i // n_pair
            rows = chunk_rows[q]
            r = r_ref[rows, lanes[p]].astype(F32)
            v = v_ref[rows, lanes[p]].astype(F32)
            kk = kk_ref[rows, lanes[p]].astype(F32)
            kd = kd_ref[rows, lanes[p]].astype(F32)
            b = b_ref[rows, lanes[p]].astype(F32)
            lw = lw_ref[rows, lanes[p]]

            cum = chunk_cum[q][:, lanes[p]]
            tot = cum[end_row:end_row + 1, :]
            e_in = jnp.exp(cum)
            e_ex = jnp.exp(cum - lw)
            e_inv = jnp.exp(-cum)
            e_end = jnp.exp(tot - cum)
            rt = _stack(r * e_in, lane_lo).astype(BF16)
            vs = _stack(v, lane_lo).astype(BF16)
            vs_s[c, p] = vs
            bk_s[c, p] = jnp.concatenate([_stack(b * e_end, lane_lo), _stack(kd * e_end, lane_lo)],
                                         axis=0).astype(BF16)
            dec_s[c, p] = jnp.exp(tot)
            ops.append(dict(
                rt=rt, vs=vs,
                at=_stack(-kk * e_ex, lane_lo).astype(BF16),
                bt=_stack(b * e_inv, lane_lo).astype(BF16),
                kt=_stack(kd * e_inv, lane_lo).astype(BF16)))

        m1 = [_dot_nt(jnp.concatenate([o['at'], o['rt']], axis=0),
                      jnp.concatenate([o['bt'], o['kt']], axis=0)) for o in ops]
        l_ab = [m[:STACK, :STACK] for m in m1]
        l_ak = [jnp.where(strict, m[:STACK, STACK:], 0.0).astype(BF16) for m in m1]
        for (c, p), m in zip(chains, m1):
            ar_s[c, p] = jnp.where(incl2, m[STACK:, :], 0.0).astype(BF16)

        inv = [eye + jnp.where(level_masks[0], l, 0.0) for l in l_ab]
        for mask in level_masks[1:]:
            inv_b = [x.astype(BF16) for x in inv]
            t = [_dot(jnp.where(mask, l, 0.0).astype(BF16), x).astype(BF16) for l, x in zip(l_ab, inv_b)]
            inv = [x + _dot(xb, y) for x, xb, y in zip(inv, inv_b, t)]
        inv_b = [x.astype(BF16) for x in inv]

        lakv = [_dot(l, o['vs']).astype(BF16) for l, o in zip(l_ak, ops)]
        for (c, p), x, o, y in zip(chains, inv_b, ops, lakv):
            wu = _dot(x, jnp.concatenate([o['at'], y], axis=1))
            wr_s[c, p] = jnp.concatenate([wu[:, :PAIR].astype(BF16), o['rt']], axis=0)
            u0_s[c, p] = wu[:, PAIR:]

    def advance(c):
        rows = pl.ds(c * CHUNK, CHUNK)
        s = [s_ref[p] for p in pairs]
        ur = [_dot_nt(wr_s[c, p], s[p].astype(BF16)) for p in pairs]
        uv = [jnp.concatenate([(ur[p][:STACK] + u0_s[c, p]).astype(BF16), vs_s[c, p]], axis=0)
              for p in pairs]
        for p in pairs:
            s_ref[p] = s[p] * dec_s[c, p] + _dot_tn(uv[p], bk_s[c, p])
        for p in pairs:
            ys = ur[p][STACK:] + _dot(ar_s[c, p], uv[p])
            y_ref[rows, lanes[p]] = ys[:CHUNK] + ys[CHUNK:]

    order = list(range(n_sub))[::-1] if reverse else list(range(n_sub))
    groups = [order[i:i + WKV_GROUP] for i in range(0, n_sub, WKV_GROUP)]
    chunk_operands(groups[0])
    for gi, group in enumerate(groups):
        if gi + 1 < len(groups):
            chunk_operands(groups[gi + 1])
        for c in group:
            advance(c)


def _wkv(r, v, kk, kd, bb, lw, reverse):
    b, t, w = r.shape
    lc = _pick_tile(t, WKV_BLOCK)
    n_t = t // lc
    n_sub = lc // CHUNK
    n_pair = w // PAIR
    tmap = (lambda bi, ti: (bi, n_t - 1 - ti, 0)) if reverse else (lambda bi, ti: (bi, ti, 0))
    seq_spec = pl.BlockSpec((None, lc, w), tmap)
    per_chain = lambda rows, cols, dtype: pltpu.VMEM((n_sub, n_pair, rows, cols), dtype)
    return pl.pallas_call(
        functools.partial(_wkv_kernel, reverse),
        grid=(b, n_t),
        in_specs=[seq_spec] * 6,
        out_specs=seq_spec,
        out_shape=jax.ShapeDtypeStruct((b, t, w), F32),
        scratch_shapes=[
            pltpu.VMEM((n_pair, PAIR, PAIR), F32),
            per_chain(2 * STACK, PAIR, BF16),
            per_chain(STACK, PAIR, F32),
            per_chain(STACK, 2 * STACK, BF16),
            per_chain(STACK, PAIR, BF16),
            per_chain(2 * STACK, PAIR, BF16),
            per_chain(1, PAIR, F32),
        ],
        compiler_params=_params("parallel", "arbitrary"),
    )(r, v, kk, kd, bb, lw)


def _nat_bias_table(rpb):
    d = jnp.array([[d for d, _ in v] for v in NAT_VARIANTS])[:, :, None]
    start = jnp.array([[s for _, s in v] for v in NAT_VARIANTS])[:, :, None]
    win_row = jnp.arange(NAT_KEY_ROWS)[None, None, :] - start
    row_ok = (win_row >= 0) & (win_row < WIN_H)
    row_rel = jnp.clip(win_row - d + (WIN_H - 1), 0, 2 * WIN_H - 2)
    cols = jnp.arange(GRID_W)
    col_start = jnp.clip(cols - WIN_W // 2, 0, GRID_W - WIN_W)
    col_ok = (cols[None, :] >= col_start[:, None]) & (cols[None, :] < col_start[:, None] + WIN_W)
    col_rel = cols[None, :] - cols[:, None] + (WIN_W - 1)
    pick = (col_ok[:, :, None] & (col_rel[:, :, None] == jnp.arange(2 * WIN_W - 1))).astype(F32)
    tbl = jnp.einsum('hvqar,jcr->hvqjac', rpb.astype(F32)[:, row_rel], pick, precision=HIGHEST)
    ok = row_ok[None, :, :, None, :, None] & col_ok[None, None, None, :, None, :]
    tbl = jnp.where(ok, tbl * LOG2_E, MASK_VALUE)
    return tbl.reshape(rpb.shape[0], len(NAT_VARIANTS), NAT_ROWS * GRID_W, NAT_KEY_ROWS * GRID_W)


def _nat_row_start(i, n_rows):
    return jnp.clip(i - WIN_H // 2, 0, n_rows - WIN_H)


def _nat_key_start(g, n_rows):
    return jnp.minimum(_nat_row_start(NAT_ROWS * g, n_rows), n_rows - NAT_KEY_ROWS)


def _nat_variant(g, n_groups):
    return jnp.where(g < 2, g, jnp.where(g >= n_groups - 2, g - (n_groups - 5), 2))


def _natten_kernel(n_rows, q_ref, k_ref, v_ref, bias_ref, o_ref):
    g = pl.program_id(1)
    n_pair = q_ref.shape[1] // PAIR
    n_q = NAT_ROWS * GRID_W
    n_keys = NAT_KEY_ROWS * GRID_W
    key_rows = pl.ds(pl.multiple_of(_nat_key_start(g, n_rows) * GRID_W, GRID_W), n_keys)
    lane_lo = lax.broadcasted_iota(jnp.int32, (n_q, PAIR), 1) < HEAD_DIM
    scale = HEAD_DIM ** -0.5 * LOG2_E
    for h0 in range(0, 2 * n_pair, NAT_HEAD_GROUP):
        heads = range(h0, h0 + NAT_HEAD_GROUP)
        lanes = {h: slice((h // 2) * PAIR, (h // 2 + 1) * PAIR) for h in heads}
        s = {}
        for h in heads:
            q = q_ref[:, lanes[h]].astype(F32) * scale
            q_h = jnp.where(lane_lo if h % 2 == 0 else ~lane_lo, q, 0.0).astype(BF16)
            s[h] = _dot_nt(q_h, k_ref[key_rows, lanes[h]]) + bias_ref[h, 0]
        e = {h: jnp.exp2(s[h] - jnp.max(s[h], axis=-1, keepdims=True)) for h in heads}
        denom = {h: jnp.sum(e[h], axis=-1, keepdims=True) for h in heads}
        outs = {h: _dot(e[h].astype(BF16), v_ref[key_rows, lanes[h]]) / denom[h] for h in heads}
        for h in range(h0, h0 + NAT_HEAD_GROUP, 2):
            o_ref[:, lanes[h]] = jnp.where(lane_lo, outs[h], outs[h + 1]).astype(o_ref.dtype)


def _natten(qkv, bias_tbl):
    b, t, w3 = qkv.shape
    w = w3 // 3
    n_rows = t // GRID_W
    n_groups = n_rows // NAT_ROWS
    n_heads = w // HEAD_DIM
    assert n_rows % NAT_ROWS == 0 and n_groups >= 5
    n_q = NAT_ROWS * GRID_W
    bias_map = lambda bi, g: (0, _nat_variant(g, n_groups), 0, 0)
    return pl.pallas_call(
        functools.partial(_natten_kernel, n_rows),
        grid=(b, n_groups),
        in_specs=[
            pl.BlockSpec((None, n_q, w), lambda bi, g: (bi, g, 0)),
            pl.BlockSpec((None, t, w), lambda bi, g: (bi, 0, 1)),
            pl.BlockSpec((None, t, w), lambda bi, g: (bi, 0, 2)),
            pl.BlockSpec((n_heads, 1, n_q, NAT_KEY_ROWS * GRID_W), bias_map),
        ],
        out_specs=pl.BlockSpec((None, n_q, w), lambda bi, g: (bi, g, 0)),
        out_shape=jax.ShapeDtypeStruct((b, t, w), BF16),
        compiler_params=_params("parallel", "arbitrary"),
    )(qkv, qkv, qkv, bias_tbl)


def _mix_out_kernel(x_ref, yf_ref, yb_ref, r_ref, v_ref, kdf_ref, kdb_ref, g_ref, onat_ref,
                    bonus_ref, gnw_ref, gnb_ref, wa_ref, wb_ref, gpost_ref, o_ref):
    n_pair = yf_ref.shape[1] // PAIR
    head_ones = _head_ones()
    acc = _dot(onat_ref[...], wb_ref[...])
    pairs = range(n_pair)
    lanes = [slice(p * PAIR, (p + 1) * PAIR) for p in pairs]
    y = [yf_ref[:, l] + yb_ref[:, l] for l in lanes]
    mean = [_split_dot(a, head_ones, 2) * (1.0 / HEAD_DIM) for a in y]
    yc = [a - m for a, m in zip(y, mean)]
    var = [_split_dot(a * a, head_ones, 2) * (1.0 / HEAD_DIM) for a in yc]
    rk = [r_ref[:, l].astype(F32) * (0.5 * (kdf_ref[:, l].astype(F32) + kdb_ref[:, l].astype(F32)))
          * bonus_ref[:, l] for l in lanes]
    bonus = [_split_dot(a, head_ones, 2) for a in rk]
    cols = []
    for p in pairs:
        l = lanes[p]
        yn = yc[p] * lax.rsqrt(var[p] + GN_EPS) * gnw_ref[:, l] + gnb_ref[:, l]
        cols.append(((yn + bonus[p] * v_ref[:, l].astype(F32)) * g_ref[:, l].astype(F32)).astype(BF16))
    for q in range(0, n_pair, 2):
        acc = acc + _dot(jnp.concatenate(cols[q:q + 2], axis=1), wa_ref[q * PAIR:(q + 2) * PAIR, :])
    o_ref[...] = x_ref[...] + _rms(acc, gpost_ref[...])


def _mix_out(x, yf, yb, r, v, kdf, kdb, g, o_nat, bonus_scale, gn_w, gn_b, w_a, w_b, g_post):
    n, d = x.shape
    w = yf.shape[1]
    tm = _pick_tile(n, 256)
    tile = lambda cols: pl.BlockSpec((tm, cols), lambda i: (i, 0))
    row = lambda cols: pl.BlockSpec((1, cols), lambda i: (0, 0))
    resident = pl.BlockSpec((w, d), lambda i: (0, 0), pipeline_mode=pl.Buffered(1))
    return pl.pallas_call(
        _mix_out_kernel,
        grid=(n // tm,),
        in_specs=[tile(d)] + [tile(w)] * 8 + [row(w)] * 3 + [resident, resident, row(d)],
        out_specs=tile(d),
        out_shape=jax.ShapeDtypeStruct((n, d), F32),
        compiler_params=_params("parallel"),
    )(x, yf, yb, r, v, kdf, kdb, g, o_nat, bonus_scale, gn_w, gn_b, w_a, w_b, g_post)


def _encoder_layer(x, p, bias_tbl):
    b, t, d = x.shape
    n = b * t
    w = p['decay_up_fwd'].shape[1]
    x = x.reshape(n, d)
    x = _ffn(x, p['ffn1_pre_g'], p['ffn1_post_g'], p['ffn1_w_gate'], p['ffn1_w_up'], p['ffn1_w_down'])

    z_rwkv, qkv = _norm_proj(x, p['mix_pre_g'], p['w_in_rwkv'], p['w_in_nat'])

    r, v, kk, kdf, kdb, bf, bb, lwf, lwb, g = _rwkv_prep(
        z_rwkv, t, p['rwkv_shift_mix'],
        p['decay_bias_fwd'], p['decay_up_fwd'], p['decay_bias_bwd'], p['decay_up_bwd'],
        p['iclr_bias_fwd'], p['iclr_up_fwd'], p['iclr_bias_bwd'], p['iclr_up_bwd'], p['gate_up'],
        p['key_norm_scale'], p['key_iclr_mix'])
    seq = lambda a: a.reshape(b, t, w)
    yf = _wkv(seq(r), seq(v), seq(kk), seq(kdf), seq(bf), seq(lwf), False)
    yb = _wkv(seq(r), seq(v), seq(kk), seq(kdb), seq(bb), seq(lwb), True)
    o_nat = _natten(qkv.reshape(b, t, -1), bias_tbl).reshape(n, -1)

    x = _mix_out(x, yf.reshape(n, w), yb.reshape(n, w), r, v, kdf, kdb, g, o_nat,
                 p['bonus_scale'], p['gn_w'], p['gn_b'],
                 p['w_out_rwkv'], p['w_out_nat'], p['mix_post_g'])
    x = _ffn(x, p['ffn2_pre_g'], p['ffn2_post_g'], p['ffn2_w_gate'], p['ffn2_w_up'], p['ffn2_w_down'])
    return x.reshape(b, t, d)


_ROW_PARAMS = ('ffn1_pre_g', 'ffn1_post_g', 'mix_pre_g', 'rwkv_shift_mix',
               'decay_bias_fwd', 'decay_bias_bwd', 'iclr_bias_fwd', 'iclr_bias_bwd',
               'key_norm_scale', 'key_iclr_mix', 'bonus_scale', 'gn_w', 'gn_b',
               'mix_post_g', 'ffn2_pre_g', 'ffn2_post_g')
_BF16_MATS = ('ffn1_w_gate', 'ffn1_w_up', 'ffn1_w_down', 'ffn2_w_gate', 'ffn2_w_up', 'ffn2_w_down',
              'decay_up_fwd', 'decay_up_bwd', 'iclr_up_fwd', 'iclr_up_bwd', 'gate_up')


def _layer_params(weights, l):
    p = {name: weights[name][l].reshape(1, -1) for name in _ROW_PARAMS}
    p.update({name: weights[name][l].astype(BF16) for name in _BF16_MATS})
    w = weights['decay_up_fwd'].shape[-1]
    rwkv_cols = 3 * w + DECAY_RANK + ICLR_RANK + GATE_RANK
    w_in = weights['w_in'][l].astype(BF16)
    p['w_in_rwkv'] = w_in[:, :rwkv_cols]
    p['w_in_nat'] = w_in[:, rwkv_cols:]
    w_out = weights['w_out'][l].astype(BF16)
    p['w_out_rwkv'] = w_out[:w]
    p['w_out_nat'] = w_out[w:]
    return p, _nat_bias_table(weights['nat_rpb'][l])


def kernel(x_prompt, x_sample, ffn1_pre_g, ffn1_post_g, ffn1_w_gate, ffn1_w_up, ffn1_w_down, mix_pre_g, w_in, rwkv_shift_mix, decay_bias_fwd, decay_up_fwd, decay_bias_bwd, decay_up_bwd, iclr_bias_fwd, iclr_up_fwd, iclr_bias_bwd, iclr_up_bwd, gate_up, key_norm_scale, key_iclr_mix, bonus_scale, gn_w, gn_b, nat_rpb, w_out, mix_post_g, ffn2_pre_g, ffn2_post_g, ffn2_w_gate, ffn2_w_up, ffn2_w_down):
    weights = dict(
        ffn1_pre_g=ffn1_pre_g, ffn1_post_g=ffn1_post_g, ffn1_w_gate=ffn1_w_gate, ffn1_w_up=ffn1_w_up,
        ffn1_w_down=ffn1_w_down, mix_pre_g=mix_pre_g, w_in=w_in, rwkv_shift_mix=rwkv_shift_mix,
        decay_bias_fwd=decay_bias_fwd, decay_up_fwd=decay_up_fwd, decay_bias_bwd=decay_bias_bwd,
        decay_up_bwd=decay_up_bwd, iclr_bias_fwd=iclr_bias_fwd, iclr_up_fwd=iclr_up_fwd,
        iclr_bias_bwd=iclr_bias_bwd, iclr_up_bwd=iclr_up_bwd, gate_up=gate_up,
        key_norm_scale=key_norm_scale, key_iclr_mix=key_iclr_mix, bonus_scale=bonus_scale,
        gn_w=gn_w, gn_b=gn_b, nat_rpb=nat_rpb, w_out=w_out, mix_post_g=mix_post_g,
        ffn2_pre_g=ffn2_pre_g, ffn2_post_g=ffn2_post_g, ffn2_w_gate=ffn2_w_gate, ffn2_w_up=ffn2_w_up,
        ffn2_w_down=ffn2_w_down)
    depth = w_in.shape[0]
    layers = [_layer_params(weights, l) for l in range(depth)]
    outs = []
    for x in (x_prompt, x_sample):
        for p, bias_tbl in layers:
            x = _encoder_layer(x, p, bias_tbl)
        outs.append(x)
    return tuple(outs)
```

```python
import functools
import math

import jax
import jax.numpy as jnp
from jax import lax
from jax.experimental import pallas as pl
from jax.experimental.pallas import tpu as pltpu

F32 = jnp.float32
BF16 = jnp.bfloat16
HIGHEST = lax.Precision.HIGHEST

HEAD_DIM = 64
GRID_W = 64
WIN_H = 8
WIN_W = 16
DECAY_RANK = 64
ICLR_RANK = 64
GATE_RANK = 128
RMS_EPS = 1e-6
GN_EPS = 64e-5
DECAY_OFFSET = 0.5
FFN_RESIDUAL = 0.5
MASK_VALUE = -1e30
LOG2_E = math.log2(math.e)

LANES = 128
HALO_ROWS = 16
EDGE_ROWS = 8
VMEM_LIMIT_BYTES = 56 * 1024 * 1024

PAIR = 2 * HEAD_DIM
CHUNK = 64
STACK = 2 * CHUNK
NORM_ROWS = 16
NAT_HEAD_GROUP = 4
NAT_ROWS = 2
NAT_KEY_ROWS = WIN_H + NAT_ROWS - 1
NAT_VARIANTS = (((0, 0), (1, 0)), ((2, 0), (3, 0)), ((4, 0), (4, 1)), ((4, 1), (5, 1)), ((6, 1), (7, 1)))
WKV_BLOCK = 512
WKV_GROUP = 2


def _params(*semantics):
    return pltpu.CompilerParams(dimension_semantics=semantics, vmem_limit_bytes=VMEM_LIMIT_BYTES)


def _dot(a, b, precision=None):
    return jnp.dot(a, b, preferred_element_type=F32, precision=precision)


def _dot_nt(a, b):
    return lax.dot_general(a, b, (((1,), (1,)), ((), ())), preferred_element_type=F32)


def _dot_tn(a, b):
    return lax.dot_general(a, b, (((0,), (0,)), ((), ())), preferred_element_type=F32)


def _rms(x, g):
    return x * lax.rsqrt(jnp.mean(x * x, axis=-1, keepdims=True) + RMS_EPS) * g


def _row_chunks(n_rows):
    step = min(n_rows, NORM_ROWS)
    return [slice(r, r + step) for r in range(0, n_rows, step)]


def _sigmoid(x):
    return 0.5 * jnp.tanh(0.5 * x) + 0.5


def _head_ones():
    return jnp.where(
        (lax.broadcasted_iota(jnp.int32, (PAIR, PAIR), 0) // HEAD_DIM)
        == (lax.broadcasted_iota(jnp.int32, (PAIR, PAIR), 1) // HEAD_DIM), 1.0, 0.0).astype(BF16)


def _split_dot(x, w, terms, lhs=False):
    acc = None
    rem = x
    for t in range(terms):
        piece = rem.astype(BF16)
        d = _dot(w, piece) if lhs else _dot(piece, w)
        acc = d if acc is None else acc + d
        if t + 1 < terms:
            rem = rem - piece.astype(F32)
    return acc


def _pick_tile(n, target):
    t = min(n, target)
    while n % t:
        t //= 2
    return t


def _ffn_kernel(x_ref, gpre_ref, gpost_ref, wg_ref, wu_ref, wd_ref, o_ref, h_ref, acc_ref):
    j = pl.program_id(1)

    @pl.when(j == 0)
    def _():
        h_ref[...] = _rms(x_ref[...], gpre_ref[...]).astype(BF16)
        acc_ref[...] = jnp.zeros_like(acc_ref)

    h = h_ref[...]
    g = _dot(h, wg_ref[...])
    u = _dot(h, wu_ref[...])
    act = (g * _sigmoid(g) * u).astype(BF16)
    acc_ref[...] += _dot(act, wd_ref[...])

    @pl.when(j == pl.num_programs(1) - 1)
    def _():
        for rows in _row_chunks(x_ref.shape[0]):
            o_ref[rows, :] = x_ref[rows, :] + FFN_RESIDUAL * _rms(acc_ref[rows, :], gpost_ref[...])


def _ffn(x, g_pre, g_post, w_gate, w_up, w_down):
    n, d = x.shape
    f = w_gate.shape[1]
    tm = _pick_tile(n, 512)
    tf = _pick_tile(f, 512)
    return pl.pallas_call(
        _ffn_kernel,
        grid=(n // tm, f // tf),
        in_specs=[
            pl.BlockSpec((tm, d), lambda i, j: (i, 0)),
            pl.BlockSpec((1, d), lambda i, j: (0, 0)),
            pl.BlockSpec((1, d), lambda i, j: (0, 0)),
            pl.BlockSpec((d, tf), lambda i, j: (0, j)),
            pl.BlockSpec((d, tf), lambda i, j: (0, j)),
            pl.BlockSpec((tf, d), lambda i, j: (j, 0)),
        ],
        out_specs=pl.BlockSpec((tm, d), lambda i, j: (i, 0)),
        out_shape=jax.ShapeDtypeStruct((n, d), F32),
        scratch_shapes=[pltpu.VMEM((tm, d), BF16), pltpu.VMEM((tm, d), F32)],
        compiler_params=_params("parallel", "arbitrary"),
    )(x, g_pre, g_post, w_gate, w_up, w_down)


def _norm_proj_kernel(x_ref, g_ref, wa_ref, wb_ref, oa_ref, ob_ref):
    h = _rms(x_ref[...], g_ref[...]).astype(BF16)
    oa_ref[...] = _dot(h, wa_ref[...]).astype(oa_ref.dtype)
    ob_ref[...] = _dot(h, wb_ref[...]).astype(ob_ref.dtype)


def _norm_proj(x, g, w_a, w_b):
    n, d = x.shape
    ca, cb = w_a.shape[1], w_b.shape[1]
    tm = _pick_tile(n, 256)
    resident = lambda cols: pl.BlockSpec((d, cols), lambda i: (0, 0), pipeline_mode=pl.Buffered(1))
    return pl.pallas_call(
        _norm_proj_kernel,
        grid=(n // tm,),
        in_specs=[
            pl.BlockSpec((tm, d), lambda i: (i, 0)),
            pl.BlockSpec((1, d), lambda i: (0, 0)),
            resident(ca),
            resident(cb),
        ],
        out_specs=[pl.BlockSpec((tm, ca), lambda i: (i, 0)), pl.BlockSpec((tm, cb), lambda i: (i, 0))],
        out_shape=[jax.ShapeDtypeStruct((n, ca), BF16), jax.ShapeDtypeStruct((n, cb), BF16)],
        compiler_params=_params("parallel"),
    )(x, g, w_a, w_b)


def _rwkv_prep_kernel(seq_len, z_ref, zp_ref, zn_ref, mu_ref,
                      dbf_ref, duf_ref, dbb_ref, dub_ref, ibf_ref, iuf_ref, ibb_ref, iub_ref, gu_ref,
                      kns_ref, mix_ref,
                      r_ref, v_ref, kk_ref, kdf_ref, kdb_ref, bf_ref, bb_ref, lwf_ref, lwb_ref, g_ref, u_ref):
    i = pl.program_id(0)
    tm = z_ref.shape[0]
    w = r_ref.shape[1]
    ti = lax.broadcasted_iota(jnp.int32, (tm, tm), 0)
    tj = lax.broadcasted_iota(jnp.int32, (tm, tm), 1)
    neighbours = jnp.where((ti == tj + 1) | (ti + 1 == tj), 1.0, 0.0).astype(BF16)
    z = z_ref[...]
    mu = mu_ref[...]
    u_ref[...] = (1.0 - mu) * z + (0.5 * mu) * _dot(neighbours, z)
    at_seq_start = (i * tm) % seq_len == 0
    at_seq_end = ((i + 1) * tm) % seq_len == 0
    prev_row = jnp.where(at_seq_start, 0.0, zp_ref[HALO_ROWS - 1:HALO_ROWS, :].astype(F32))
    next_row = jnp.where(at_seq_end, 0.0, zn_ref[0:1, :].astype(F32))
    edge = lax.broadcasted_iota(jnp.int32, (EDGE_ROWS, 1), 0)
    u_ref[0:EDGE_ROWS, :] += jnp.where(edge == 0, 0.5 * mu * prev_row, 0.0)
    u_ref[tm - EDGE_ROWS:tm, :] += jnp.where(edge == EDGE_ROWS - 1, 0.5 * mu * next_row, 0.0)

    r_ref[...] = u_ref[:, 0:w].astype(r_ref.dtype)
    v_ref[...] = u_ref[:, 2 * w:3 * w].astype(v_ref.dtype)
    o = 3 * w
    dec_lo = jnp.tanh(u_ref[:, o:o + DECAY_RANK]).astype(BF16)
    o += DECAY_RANK
    iclr_lo = u_ref[:, o:o + ICLR_RANK].astype(BF16)
    o += ICLR_RANK
    gate_lo = _sigmoid(u_ref[:, o:o + GATE_RANK]).astype(BF16)

    scale = -math.exp(-DECAY_OFFSET)
    lwf_ref[...] = scale * _sigmoid(dbf_ref[...] + _dot(dec_lo, duf_ref[...]))
    lwb_ref[...] = scale * _sigmoid(dbb_ref[...] + _dot(dec_lo, dub_ref[...]))
    g_ref[...] = _dot(gate_lo, gu_ref[...]).astype(g_ref.dtype)

    head_ones = _head_ones()
    for p in range(w // PAIR):
        lanes = slice(p * PAIR, (p + 1) * PAIR)
        k = u_ref[:, w + p * PAIR:w + (p + 1) * PAIR]
        kn = k * kns_ref[:, lanes]
        kk = kn * lax.rsqrt(jnp.maximum(_split_dot(kn * kn, head_ones, 2), 1e-24))
        kk_ref[:, lanes] = kk.astype(kk_ref.dtype)
        mix = mix_ref[:, lanes]
        for ib_ref, iu_ref, kd_ref, b_ref in ((ibf_ref, iuf_ref, kdf_ref, bf_ref),
                                              (ibb_ref, iub_ref, kdb_ref, bb_ref)):
            al = _sigmoid(ib_ref[:, lanes] + _dot(iclr_lo, iu_ref[:, lanes]))
            kd_ref[:, lanes] = (k * (1.0 + (al - 1.0) * mix)).astype(kd_ref.dtype)
            b_ref[:, lanes] = (kk * al).astype(b_ref.dtype)


def _rwkv_prep(z, seq_len, mu, dbf, duf, dbb, dub, ibf, iuf, ibb, iub, gate_up, kns, mix):
    n, c = z.shape
    assert z.dtype == BF16
    w = duf.shape[1]
    tm = _pick_tile(seq_len, 256)
    hb = tm // HALO_ROWS
    last_hb = n // HALO_ROWS - 1
    row_spec = lambda cols: pl.BlockSpec((1, cols), lambda i: (0, 0))
    mat_spec = lambda rows: pl.BlockSpec((rows, w), lambda i: (0, 0))
    out_spec = pl.BlockSpec((tm, w), lambda i: (i, 0))
    outs = [jax.ShapeDtypeStruct((n, w), dt) for dt in (BF16,) * 7 + (F32, F32, BF16)]
    return pl.pallas_call(
        functools.partial(_rwkv_prep_kernel, seq_len),
        grid=(n // tm,),
        in_specs=[
            pl.BlockSpec((tm, c), lambda i: (i, 0)),
            pl.BlockSpec((HALO_ROWS, c), lambda i: (jnp.maximum(i * hb - 1, 0), 0)),
            pl.BlockSpec((HALO_ROWS, c), lambda i: (jnp.minimum((i + 1) * hb, last_hb), 0)),
            row_spec(c),
            row_spec(w), mat_spec(DECAY_RANK), row_spec(w), mat_spec(DECAY_RANK),
            row_spec(w), mat_spec(ICLR_RANK), row_spec(w), mat_spec(ICLR_RANK),
            mat_spec(GATE_RANK), row_spec(w), row_spec(w),
        ],
        out_specs=[out_spec] * 10,
        out_shape=outs,
        scratch_shapes=[pltpu.VMEM((tm, c), F32)],
        compiler_params=_params("parallel"),
    )(z, z, z, mu, dbf, duf, dbb, dub, ibf, iuf, ibb, iub, gate_up, kns, mix)


def _stack(x, lane_lo):
    return jnp.concatenate([jnp.where(lane_lo, x, 0.0), jnp.where(lane_lo, 0.0, x)], axis=0)


def _wkv_kernel(reverse, r_ref, v_ref, kk_ref, kd_ref, b_ref, lw_ref, y_ref,
                s_ref, wr_s, u0_s, ar_s, vs_s, bk_s, dec_s):
    tc = pl.program_id(1)
    n_sub = r_ref.shape[0] // CHUNK
    n_pair = r_ref.shape[1] // PAIR

    @pl.when(tc == 0)
    def _():
        s_ref[...] = jnp.zeros_like(s_ref)

    ii = lax.broadcasted_iota(jnp.int32, (STACK, STACK), 0)
    jj = lax.broadcasted_iota(jnp.int32, (STACK, STACK), 1)
    ti = ii % CHUNK
    tj = jj % CHUNK
    same_head = (ii // CHUNK) == (jj // CHUNK)
    ii2 = lax.broadcasted_iota(jnp.int32, (STACK, 2 * STACK), 0)
    jj2 = lax.broadcasted_iota(jnp.int32, (STACK, 2 * STACK), 1)
    same_head2 = (ii2 // CHUNK) == ((jj2 % STACK) // CHUNK)
    if reverse:
        strict = same_head & (ti < tj)
        incl2 = same_head2 & (ii2 % CHUNK <= jj2 % CHUNK)
    else:
        strict = same_head & (ti > tj)
        incl2 = same_head2 & (ii2 % CHUNK >= jj2 % CHUNK)
    ci = lax.broadcasted_iota(jnp.int32, (CHUNK, CHUNK), 0)
    cj = lax.broadcasted_iota(jnp.int32, (CHUNK, CHUNK), 1)
    cum_tri = jnp.where((ci <= cj) if reverse else (ci >= cj), 1.0, 0.0).astype(BF16)
    eye = jnp.where(ii == jj, 1.0, 0.0).astype(F32)
    lane_lo = lax.broadcasted_iota(jnp.int32, (CHUNK, PAIR), 1) < HEAD_DIM
    end_row = 0 if reverse else CHUNK - 1

    level_masks = []
    m = 1
    while m < CHUNK:
        level_masks.append(strict & (ti // (2 * m) == tj // (2 * m)) & (ti // m != tj // m))
        m *= 2

    pairs = range(n_pair)
    lanes = [slice(p * PAIR, (p + 1) * PAIR) for p in pairs]

    def chunk_operands(chunk_ids):
        chunk_rows = [pl.ds(c * CHUNK, CHUNK) for c in chunk_ids]
        chunk_cum = [_split_dot(lw_ref[rows, :], cum_tri, 3, lhs=True) for rows in chunk_rows]
        chains = [(chunk_ids[q], p) for q in range(WKV_GROUP) for p in pairs]
        ops = []
        for i, (c, p) in enumerate(chains):
            q = i // n_pair
            rows = chunk_rows[q]
            r = r_ref[rows, lanes[p]].astype(F32)
            v = v_ref[rows, lanes[p]].astype(F32)
            kk = kk_ref[rows, lanes[p]].astype(F32)
            kd = kd_ref[rows, lanes[p]].astype(F32)
            b = b_ref[rows, lanes[p]].astype(F32)
            lw = lw_ref[rows, lanes[p]]

            cum = chunk_cum[q][:, lanes[p]]
            tot = cum[end_row:end_row + 1, :]
            e_in = jnp.exp(cum)
            e_ex = jnp.exp(cum - lw)
            e_inv = jnp.exp(-cum)
            e_end = jnp.exp(tot - cum)
            rt = _stack(r * e_in, lane_lo).astype(BF16)
            vs = _stack(v, lane_lo).astype(BF16)
            vs_s[c, p] = vs
            bk_s[c, p] = jnp.concatenate([_stack(b * e_end, lane_lo), _stack(kd * e_end, lane_lo)],
                                         axis=0).astype(BF16)
            dec_s[c, p] = jnp.exp(tot)
            ops.append(dict(
                rt=rt, vs=vs,
                at=_stack(-kk * e_ex, lane_lo).astype(BF16),
                bt=_stack(b * e_inv, lane_lo).astype(BF16),
                kt=_stack(kd * e_inv, lane_lo).astype(BF16)))

        m1 = [_dot_nt(jnp.concatenate([o['at'], o['rt']], axis=0),
                      jnp.concatenate([o['bt'], o['kt']], axis=0)) for o in ops]
        l_ab = [m[:STACK, :STACK] for m in m1]
        l_ak = [jnp.where(strict, m[:STACK, STACK:], 0.0).astype(BF16) for m in m1]
        for (c, p), m in zip(chains, m1):
            ar_s[c, p] = jnp.where(incl2, m[STACK:, :], 0.0).astype(BF16)

        inv = [eye + jnp.where(level_masks[0], l, 0.0) for l in l_ab]
        for mask in level_masks[1:]:
            inv_b = [x.astype(BF16) for x in inv]
            t = [_dot(jnp.where(mask, l, 0.0).astype(BF16), x).astype(BF16) for l, x in zip(l_ab, inv_b)]
            inv = [x + _dot(xb, y) for x, xb, y in zip(inv, inv_b, t)]
        inv_b = [x.astype(BF16) for x in inv]

        lakv = [_dot(l, o['vs']).astype(BF16) for l, o in zip(l_ak, ops)]
        for (c, p), x, o, y in zip(chains, inv_b, ops, lakv):
            wu = _dot(x, jnp.concatenate([o['at'], y], axis=1))
            wr_s[c, p] = jnp.concatenate([wu[:, :PAIR].astype(BF16), o['rt']], axis=0)
            u0_s[c, p] = wu[:, PAIR:]

    def advance(c):
        rows = pl.ds(c * CHUNK, CHUNK)
        s = [s_ref[p] for p in pairs]
        ur = [_dot_nt(wr_s[c, p], s[p].astype(BF16)) for p in pairs]
        uv = [jnp.concatenate([(ur[p][:STACK] + u0_s[c, p]).astype(BF16), vs_s[c, p]], axis=0)
              for p in pairs]
        for p in pairs:
            s_ref[p] = s[p] * dec_s[c, p] + _dot_tn(uv[p], bk_s[c, p])
        for p in pairs:
            ys = ur[p][STACK:] + _dot(ar_s[c, p], uv[p])
            y_ref[rows, lanes[p]] = ys[:CHUNK] + ys[CHUNK:]

    order = list(range(n_sub))[::-1] if reverse else list(range(n_sub))
    groups = [order[i:i + WKV_GROUP] for i in range(0, n_sub, WKV_GROUP)]
    chunk_operands(groups[0])
    for gi, group in enumerate(groups):
        if gi + 1 < len(groups):
            chunk_operands(groups[gi + 1])
        for c in group:
            advance(c)


def _wkv(r, v, kk, kd, bb, lw, reverse):
    b, t, w = r.shape
    lc = _pick_tile(t, WKV_BLOCK)
    n_t = t // lc
    n_sub = lc // CHUNK
    n_pair = w // PAIR
    tmap = (lambda bi, ti: (bi, n_t - 1 - ti, 0)) if reverse else (lambda bi, ti: (bi, ti, 0))
    seq_spec = pl.BlockSpec((None, lc, w), tmap)
    per_chain = lambda rows, cols, dtype: pltpu.VMEM((n_sub, n_pair, rows, cols), dtype)
    return pl.pallas_call(
        functools.partial(_wkv_kernel, reverse),
        grid=(b, n_t),
        in_specs=[seq_spec] * 6,
        out_specs=seq_spec,
        out_shape=jax.ShapeDtypeStruct((b, t, w), F32),
        scratch_shapes=[
            pltpu.VMEM((n_pair, PAIR, PAIR), F32),
            per_chain(2 * STACK, PAIR, BF16),
            per_chain(STACK, PAIR, F32),
            per_chain(STACK, 2 * STACK, BF16),
            per_chain(STACK, PAIR, BF16),
            per_chain(2 * STACK, PAIR, BF16),
            per_chain(1, PAIR, F32),
        ],
        compiler_params=_params("parallel", "arbitrary"),
    )(r, v, kk, kd, bb, lw)


def _nat_bias_table(rpb):
    d = jnp.array([[d for d, _ in v] for v in NAT_VARIANTS])[:, :, None]
    start = jnp.array([[s for _, s in v] for v in NAT_VARIANTS])[:, :, None]
    win_row = jnp.arange(NAT_KEY_ROWS)[None, None, :] - start
    row_ok = (win_row >= 0) & (win_row < WIN_H)
    row_rel = jnp.clip(win_row - d + (WIN_H - 1), 0, 2 * WIN_H - 2)
    cols = jnp.arange(GRID_W)
    col_start = jnp.clip(cols - WIN_W // 2, 0, GRID_W - WIN_W)
    col_ok = (cols[None, :] >= col_start[:, None]) & (cols[None, :] < col_start[:, None] + WIN_W)
    col_rel = cols[None, :] - cols[:, None] + (WIN_W - 1)
    pick = (col_ok[:, :, None] & (col_rel[:, :, None] == jnp.arange(2 * WIN_W - 1))).astype(F32)
    tbl = jnp.einsum('hvqar,jcr->hvqjac', rpb.astype(F32)[:, row_rel], pick, precision=HIGHEST)
    ok = row_ok[None, :, :, None, :, None] & col_ok[None, None, None, :, None, :]
    tbl = jnp.where(ok, tbl * LOG2_E, MASK_VALUE)
    return tbl.reshape(rpb.shape[0], len(NAT_VARIANTS), NAT_ROWS * GRID_W, NAT_KEY_ROWS * GRID_W)


def _nat_row_start(i, n_rows):
    return jnp.clip(i - WIN_H // 2, 0, n_rows - WIN_H)


def _nat_key_start(g, n_rows):
    return jnp.minimum(_nat_row_start(NAT_ROWS * g, n_rows), n_rows - NAT_KEY_ROWS)


def _nat_variant(g, n_groups):
    return jnp.where(g < 2, g, jnp.where(g >= n_groups - 2, g - (n_groups - 5), 2))


def _natten_kernel(n_rows, q_ref, k_ref, v_ref, bias_ref, o_ref):
    g = pl.program_id(1)
    n_pair = q_ref.shape[1] // PAIR
    n_q = NAT_ROWS * GRID_W
    n_keys = NAT_KEY_ROWS * GRID_W
    key_rows = pl.ds(pl.multiple_of(_nat_key_start(g, n_rows) * GRID_W, GRID_W), n_keys)
    lane_lo = lax.broadcasted_iota(jnp.int32, (n_q, PAIR), 1) < HEAD_DIM
    scale = HEAD_DIM ** -0.5 * LOG2_E
    for h0 in range(0, 2 * n_pair, NAT_HEAD_GROUP):
        heads = range(h0, h0 + NAT_HEAD_GROUP)
        lanes = {h: slice((h // 2) * PAIR, (h // 2 + 1) * PAIR) for h in heads}
        s = {}
        for h in heads:
            q = q_ref[:, lanes[h]].astype(F32) * scale
            q_h = jnp.where(lane_lo if h % 2 == 0 else ~lane_lo, q, 0.0).astype(BF16)
            s[h] = _dot_nt(q_h, k_ref[key_rows, lanes[h]]) + bias_ref[h, 0]
        e = {h: jnp.exp2(s[h] - jnp.max(s[h], axis=-1, keepdims=True)) for h in heads}
        denom = {h: jnp.sum(e[h], axis=-1, keepdims=True) for h in heads}
        outs = {h: _dot(e[h].astype(BF16), v_ref[key_rows, lanes[h]]) / denom[h] for h in heads}
        for h in range(h0, h0 + NAT_HEAD_GROUP, 2):
            o_ref[:, lanes[h]] = jnp.where(lane_lo, outs[h], outs[h + 1]).astype(o_ref.dtype)


def _natten(qkv, bias_tbl):
    b, t, w3 = qkv.shape
    w = w3 // 3
    n_rows = t // GRID_W
    n_groups = n_rows // NAT_ROWS
    n_heads = w // HEAD_DIM
    assert n_rows % NAT_ROWS == 0 and n_groups >= 5
    n_q = NAT_ROWS * GRID_W
    bias_map = lambda bi, g: (0, _nat_variant(g, n_groups), 0, 0)
    return pl.pallas_call(
        functools.partial(_natten_kernel, n_rows),
        grid=(b, n_groups),
        in_specs=[
            pl.BlockSpec((None, n_q, w), lambda bi, g: (bi, g, 0)),
            pl.BlockSpec((None, t, w), lambda bi, g: (bi, 0, 1)),
            pl.BlockSpec((None, t, w), lambda bi, g: (bi, 0, 2)),
            pl.BlockSpec((n_heads, 1, n_q, NAT_KEY_ROWS * GRID_W), bias_map),
        ],
        out_specs=pl.BlockSpec((None, n_q, w), lambda bi, g: (bi, g, 0)),
        out_shape=jax.ShapeDtypeStruct((b, t, w), BF16),
        compiler_params=_params("parallel", "arbitrary"),
    )(qkv, qkv, qkv, bias_tbl)


def _mix_out_kernel(x_ref, yf_ref, yb_ref, r_ref, v_ref, kdf_ref, kdb_ref, g_ref, onat_ref,
                    bonus_ref, gnw_ref, gnb_ref, wa_ref, wb_ref, gpost_ref, o_ref):
    n_pair = yf_ref.shape[1] // PAIR
    head_ones = _head_ones()
    acc = _dot(onat_ref[...], wb_ref[...])
    pairs = range(n_pair)
    lanes = [slice(p * PAIR, (p + 1) * PAIR) for p in pairs]
    y = [yf_ref[:, l] + yb_ref[:, l] for l in lanes]
    mean = [_split_dot(a, head_ones, 2) * (1.0 / HEAD_DIM) for a in y]
    yc = [a - m for a, m in zip(y, mean)]
    var = [_split_dot(a * a, head_ones, 2) * (1.0 / HEAD_DIM) for a in yc]
    rk = [r_ref[:, l].astype(F32) * (0.5 * (kdf_ref[:, l].astype(F32) + kdb_ref[:, l].astype(F32)))
          * bonus_ref[:, l] for l in lanes]
    bonus = [_split_dot(a, head_ones, 2) for a in rk]
    cols = []
    for p in pairs:
        l = lanes[p]
        yn = yc[p] * lax.rsqrt(var[p] + GN_EPS) * gnw_ref[:, l] + gnb_ref[:, l]
        cols.append(((yn + bonus[p] * v_ref[:, l].astype(F32)) * g_ref[:, l].astype(F32)).astype(BF16))
    for q in range(0, n_pair, 2):
        acc = acc + _dot(jnp.concatenate(cols[q:q + 2], axis=1), wa_ref[q * PAIR:(q + 2) * PAIR, :])
    o_ref[...] = x_ref[...] + _rms(acc, gpost_ref[...])


def _mix_out(x, yf, yb, r, v, kdf, kdb, g, o_nat, bonus_scale, gn_w, gn_b, w_a, w_b, g_post):
    n, d = x.shape
    w = yf.shape[1]
    tm = _pick_tile(n, 256)
    tile = lambda cols: pl.BlockSpec((tm, cols), lambda i: (i, 0))
    row = lambda cols: pl.BlockSpec((1, cols), lambda i: (0, 0))
    resident = pl.BlockSpec((w, d), lambda i: (0, 0), pipeline_mode=pl.Buffered(1))
    return pl.pallas_call(
        _mix_out_kernel,
        grid=(n // tm,),
        in_specs=[tile(d)] + [tile(w)] * 8 + [row(w)] * 3 + [resident, resident, row(d)],
        out_specs=tile(d),
        out_shape=jax.ShapeDtypeStruct((n, d), F32),
        compiler_params=_params("parallel"),
    )(x, yf, yb, r, v, kdf, kdb, g, o_nat, bonus_scale, gn_w, gn_b, w_a, w_b, g_post)


def _encoder_layer(x, p, bias_tbl):
    b, t, d = x.shape
    n = b * t
    w = p['decay_up_fwd'].shape[1]
    x = x.reshape(n, d)
    x = _ffn(x, p['ffn1_pre_g'], p['ffn1_post_g'], p['ffn1_w_gate'], p['ffn1_w_up'], p['ffn1_w_down'])

    z_rwkv, qkv = _norm_proj(x, p['mix_pre_g'], p['w_in_rwkv'], p['w_in_nat'])

    r, v, kk, kdf, kdb, bf, bb, lwf, lwb, g = _rwkv_prep(
        z_rwkv, t, p['rwkv_shift_mix'],
        p['decay_bias_fwd'], p['decay_up_fwd'], p['decay_bias_bwd'], p['decay_up_bwd'],
        p['iclr_bias_fwd'], p['iclr_up_fwd'], p['iclr_bias_bwd'], p['iclr_up_bwd'], p['gate_up'],
        p['key_norm_scale'], p['key_iclr_mix'])
    seq = lambda a: a.reshape(b, t, w)
    yf = _wkv(seq(r), seq(v), seq(kk), seq(kdf), seq(bf), seq(lwf), False)
    yb = _wkv(seq(r), seq(v), seq(kk), seq(kdb), seq(bb), seq(lwb), True)
    o_nat = _natten(qkv.reshape(b, t, -1), bias_tbl).reshape(n, -1)

    x = _mix_out(x, yf.reshape(n, w), yb.reshape(n, w), r, v, kdf, kdb, g, o_nat,
                 p['bonus_scale'], p['gn_w'], p['gn_b'],
                 p['w_out_rwkv'], p['w_out_nat'], p['mix_post_g'])
    x = _ffn(x, p['ffn2_pre_g'], p['ffn2_post_g'], p['ffn2_w_gate'], p['ffn2_w_up'], p['ffn2_w_down'])
    return x.reshape(b, t, d)


_ROW_PARAMS = ('ffn1_pre_g', 'ffn1_post_g', 'mix_pre_g', 'rwkv_shift_mix',
               'decay_bias_fwd', 'decay_bias_bwd', 'iclr_bias_fwd', 'iclr_bias_bwd',
               'key_norm_scale', 'key_iclr_mix', 'bonus_scale', 'gn_w', 'gn_b',
               'mix_post_g', 'ffn2_pre_g', 'ffn2_post_g')
_BF16_MATS = ('ffn1_w_gate', 'ffn1_w_up', 'ffn1_w_down', 'ffn2_w_gate', 'ffn2_w_up', 'ffn2_w_down',
              'decay_up_fwd', 'decay_up_bwd', 'iclr_up_fwd', 'iclr_up_bwd', 'gate_up')


def _layer_params(weights, l):
    p = {name: weights[name][l].reshape(1, -1) for name in _ROW_PARAMS}
    p.update({name: weights[name][l].astype(BF16) for name in _BF16_MATS})
    w = weights['decay_up_fwd'].shape[-1]
    rwkv_cols = 3 * w + DECAY_RANK + ICLR_RANK + GATE_RANK
    w_in = weights['w_in'][l].astype(BF16)
    p['w_in_rwkv'] = w_in[:, :rwkv_cols]
    p['w_in_nat'] = w_in[:, rwkv_cols:]
    w_out = weights['w_out'][l].astype(BF16)
    p['w_out_rwkv'] = w_out[:w]
    p['w_out_nat'] = w_out[w:]
    return p, _nat_bias_table(weights['nat_rpb'][l])


def kernel(x_prompt, x_sample, ffn1_pre_g, ffn1_post_g, ffn1_w_gate, ffn1_w_up, ffn1_w_down, mix_pre_g, w_in, rwkv_shift_mix, decay_bias_fwd, decay_up_fwd, decay_bias_bwd, decay_up_bwd, iclr_bias_fwd, iclr_up_fwd, iclr_bias_bwd, iclr_up_bwd, gate_up, key_norm_scale, key_iclr_mix, bonus_scale, gn_w, gn_b, nat_rpb, w_out, mix_post_g, ffn2_pre_g, ffn2_post_g, ffn2_w_gate, ffn2_w_up, ffn2_w_down):
    weights = dict(
        ffn1_pre_g=ffn1_pre_g, ffn1_post_g=ffn1_post_g, ffn1_w_gate=ffn1_w_gate, ffn1_w_up=ffn1_w_up,
        ffn1_w_down=ffn1_w_down, mix_pre_g=mix_pre_g, w_in=w_in, rwkv_shift_mix=rwkv_shift_mix,
        decay_bias_fwd=decay_bias_fwd, decay_up_fwd=decay_up_fwd, decay_bias_bwd=decay_bias_bwd,
        decay_up_bwd=decay_up_bwd, iclr_bias_fwd=iclr_bias_fwd, iclr_up_fwd=iclr_up_fwd,
        iclr_bias_bwd=iclr_bias_bwd, iclr_up_bwd=iclr_up_bwd, gate_up=gate_up,
        key_norm_scale=key_norm_scale, key_iclr_mix=key_iclr_mix, bonus_scale=bonus_scale,
        gn_w=gn_w, gn_b=gn_b, nat_rpb=nat_rpb, w_out=w_out, mix_post_g=mix_post_g,
        ffn2_pre_g=ffn2_pre_g, ffn2_post_g=ffn2_post_g, ffn2_w_gate=ffn2_w_gate, ffn2_w_up=ffn2_w_up,
        ffn2_w_down=ffn2_w_down)
    depth = w_in.shape[0]
    layers = [_layer_params(weights, l) for l in range(depth)]
    outs = []
    for x in (x_prompt, x_sample):
        for p, bias_tbl in layers:
            x = _encoder_layer(x, p, bias_tbl)
        outs.append(x)
    return tuple(outs)
```
